```python
import math
import jax
import jax.numpy as jnp
from jax import lax
import numpy as np

D_MODEL = 2048
BATCH = 8
SEQ = 2048
DEPTH = 2
DEC_BATCH = 128
DEC_SEQ = 8
PAST_LEN = 2048
PAGE_SIZE = 128

GLA_HEADS = 4
GLA_DK = D_MODEL // 2 // GLA_HEADS
GLA_DV = D_MODEL // GLA_HEADS
GLA_RANK = 16
GLA_TAU = 16.0
GLA_CHUNK = 16
MOBA_HEADS = 16
MOBA_DH = D_MODEL // MOBA_HEADS
MOBA_BLOCK = 256
MOBA_TOPK = 3
MOBA_QCHUNK = 16
D_FF = 11 * D_MODEL // 4
N_EXPERTS = 8
TOP_K = 2
D_FF_EXPERT = 11 * D_MODEL // 16
N_DENSE = (DEPTH + 1) // 2
N_MOE = DEPTH // 2
RMS_EPS = 1e-6

GLA_QK_W = GLA_HEADS * GLA_DK
GLA_V_W = GLA_HEADS * GLA_DV
MOBA_W = MOBA_HEADS * MOBA_DH
IN_SPLITS = (GLA_QK_W, GLA_QK_W, GLA_V_W, GLA_RANK, GLA_V_W, MOBA_W, MOBA_W, MOBA_W, D_MODEL, D_MODEL)
IN_COLS = sum(IN_SPLITS)

kernel_name = 'hybrid_gla_moba_decoder_step'


def rmsnorm(x, g):
    x32 = x.astype(jnp.float32)
    y = x32 * lax.rsqrt(jnp.mean(x32 * x32, axis=-1, keepdims=True) + RMS_EPS)
    return (y * g.astype(jnp.float32)).astype(x.dtype)


def mixer_inputs(xn, w_in_l, w_a2, b_a):
    n, t, _ = xn.shape
    h = xn @ w_in_l
    offs = np.cumsum(np.array(IN_SPLITS))[:-1].tolist()
    q_g, k_g, v_g, a_lr, r_g, q_m, k_m, v_m, gate_a, gate_b = jnp.split(h, offs, axis=-1)
    q_g = q_g.reshape(n, t, GLA_HEADS, GLA_DK) * (GLA_DK ** -0.5)
    k_g = k_g.reshape(n, t, GLA_HEADS, GLA_DK)
    v_g = v_g.reshape(n, t, GLA_HEADS, GLA_DV)
    log_a = jax.nn.log_sigmoid((a_lr @ w_a2 + b_a).astype(jnp.float32)) / GLA_TAU
    log_a = log_a.reshape(n, t, GLA_HEADS, GLA_DK)
    q_m = q_m.reshape(n, t, MOBA_HEADS, MOBA_DH)
    k_m = k_m.reshape(n, t, MOBA_HEADS, MOBA_DH)
    v_m = v_m.reshape(n, t, MOBA_HEADS, MOBA_DH)
    return q_g, k_g, v_g, log_a, r_g, q_m, k_m, v_m, gate_a, gate_b


def gla_chunk_step(S, chunk):
    q, k, v, la = chunk
    q32 = q.astype(jnp.float32)
    k32 = k.astype(jnp.float32)
    v32 = v.astype(jnp.float32)
    c = q.shape[1]
    b = jnp.cumsum(la, axis=1)
    o_inter = jnp.einsum('nthk,nhkv->nthv', q32 * jnp.exp(b), S)
    causal = jnp.tril(jnp.ones((c, c), dtype=bool))[None, :, :, None, None]
    decay = jnp.exp(jnp.where(causal, b[:, :, None] - b[:, None, :], -jnp.inf))
    attn = jnp.einsum('nthk,nshk,ntshk->nhts', q32, k32, decay)
    o_intra = jnp.einsum('nhts,nshv->nthv', attn, v32)
    b_last = b[:, -1]
    S_new = jnp.exp(b_last)[..., None] * S + jnp.einsum('nshk,nshv->nhkv', k32 * jnp.exp(b_last[:, None] - b), v32)
    return S_new, o_inter + o_intra


def gla_prompt(q, k, v, la):
    n, t = q.shape[:2]
    nc = t // GLA_CHUNK

    def to_chunks(a):
        return jnp.moveaxis(a.reshape(n, nc, GLA_CHUNK, *a.shape[2:]), 1, 0)

    S0 = jnp.zeros((n, GLA_HEADS, GLA_DK, GLA_DV), jnp.float32)
    S_fin, o = lax.scan(gla_chunk_step, S0, (to_chunks(q), to_chunks(k), to_chunks(v), to_chunks(la)))
    o = jnp.moveaxis(o, 0, 1).reshape(n, t, GLA_HEADS, GLA_DV)
    return o, S_fin


def gla_post(o, r, gain):
    n, t = o.shape[:2]
    o = o * lax.rsqrt(jnp.mean(o * o, axis=-1, keepdims=True) + RMS_EPS) * gain.astype(jnp.float32)
    o = o.reshape(n, t, GLA_V_W) * jax.nn.silu(r.astype(jnp.float32))
    return o.astype(r.dtype)


def moba_prepare(k_all, v_all):
    L = k_all.shape[0]
    nb = -(-L // MOBA_BLOCK)
    pad = nb * MOBA_BLOCK - L

    def blocks(a):
        a = jnp.pad(a, ((0, pad), (0, 0), (0, 0)))
        return a.reshape(nb, MOBA_BLOCK, MOBA_HEADS, MOBA_DH).transpose(2, 0, 1, 3)

    kbh = blocks(k_all)
    vbh = blocks(v_all)
    kmean = jnp.mean(kbh.astype(jnp.float32), axis=2)
    return kbh, vbh, kmean


def moba_chunk(q_c, pos_c, kbh, vbh, kmean):
    qc = q_c.shape[0]
    nb = kbh.shape[1]
    scale = MOBA_DH ** -0.5
    q32 = q_c.astype(jnp.float32)
    gate = jnp.einsum('qhd,hnd->qhn', q32, kmean)
    own = pos_c // MOBA_BLOCK
    past = jnp.arange(nb)[None, None, :] < own[:, None, None]
    gate = jnp.where(past, gate, -jnp.inf)
    if nb < MOBA_TOPK:
        gate = jnp.pad(gate, ((0, 0), (0, 0), (0, MOBA_TOPK - nb)), constant_values=-jnp.inf)
    top_s, top_i = lax.top_k(gate, MOBA_TOPK)
    valid = jnp.isfinite(top_s)
    sel = jnp.minimum(top_i, nb - 1)
    head = jnp.arange(MOBA_HEADS)[None, :, None]
    k_sel = kbh[head, sel]
    v_sel = vbh[head, sel]
    n_sel = MOBA_TOPK * MOBA_BLOCK
    s_sel = jnp.einsum('qhd,qhjbd->qhjb', q32, k_sel.astype(jnp.float32)) * scale
    s_sel = jnp.where(valid[..., None], s_sel, -jnp.inf).reshape(qc, MOBA_HEADS, n_sel)
    c0 = pos_c[0] // MOBA_BLOCK
    k_own = lax.dynamic_index_in_dim(kbh, c0, axis=1, keepdims=False)
    v_own = lax.dynamic_index_in_dim(vbh, c0, axis=1, keepdims=False)
    s_own = jnp.einsum('qhd,hbd->qhb', q32, k_own.astype(jnp.float32)) * scale
    key_pos = c0 * MOBA_BLOCK + jnp.arange(MOBA_BLOCK)
    s_own = jnp.where(key_pos[None, None, :] <= pos_c[:, None, None], s_own, -jnp.inf)
    p = jax.nn.softmax(jnp.concatenate([s_sel, s_own], axis=-1), axis=-1)
    out = jnp.einsum('qhj,qhjd->qhd', p[..., :n_sel], v_sel.reshape(qc, MOBA_HEADS, n_sel, MOBA_DH).astype(jnp.float32))
    out = out + jnp.einsum('qhb,hbd->qhd', p[..., n_sel:], v_own.astype(jnp.float32))
    return out.astype(q_c.dtype)


def moba_prompt(q, k, v):
    def one_seq(args):
        qs, ks, vs = args
        kbh, vbh, kmean = moba_prepare(ks, vs)
        t = qs.shape[0]
        nqc = t // MOBA_QCHUNK
        qr = qs.reshape(nqc, MOBA_QCHUNK, MOBA_HEADS, MOBA_DH)
        pr = jnp.arange(t, dtype=jnp.int32).reshape(nqc, MOBA_QCHUNK)
        out = lax.map(lambda a: moba_chunk(a[0], a[1], kbh, vbh, kmean), (qr, pr))
        return out.reshape(t, MOBA_HEADS, MOBA_DH)

    return lax.map(one_seq, (q, k, v))


def moba_sample(q, k_new, v_new, pool_k, pool_v, page_table):
    def one_seq(args):
        qs, kn, vn, pt = args
        k_past = pool_k[pt].reshape(-1, MOBA_HEADS, MOBA_DH)
        v_past = pool_v[pt].reshape(-1, MOBA_HEADS, MOBA_DH)
        past_len = k_past.shape[0]
        k_all = jnp.concatenate([k_past, kn.astype(k_past.dtype)], axis=0)
        v_all = jnp.concatenate([v_past, vn.astype(v_past.dtype)], axis=0)
        kbh, vbh, kmean = moba_prepare(k_all, v_all)
        pos = past_len + jnp.arange(qs.shape[0], dtype=jnp.int32)
        return moba_chunk(qs, pos, kbh, vbh, kmean)

    return lax.map(one_seq, (q, k_new, v_new, page_table))


def mix_out(og, om, gate_a, gate_b, w_pa, w_pb, w_o):
    n, t = om.shape[:2]
    ya = (og @ w_pa).astype(jnp.float32)
    yb = (om.reshape(n, t, MOBA_W) @ w_pb).astype(jnp.float32)
    m = jax.nn.sigmoid(gate_a.astype(jnp.float32)) * ya + jax.nn.sigmoid(gate_b.astype(jnp.float32)) * yb
    return m.astype(og.dtype) @ w_o


def swiglu(h, wg, wu, wd):
    return (jax.nn.silu(h @ wg) * (h @ wu)) @ wd


def moe(h, router, wg, wu, wd):
    logits = (h @ router).astype(jnp.float32)
    top_v, top_i = lax.top_k(logits, TOP_K)
    w = jax.nn.softmax(top_v, axis=-1)
    gates = jnp.sum(jax.nn.one_hot(top_i, N_EXPERTS, dtype=jnp.float32) * w[..., None], axis=1)
    out = jnp.zeros(h.shape, jnp.float32)
    for e in range(N_EXPERTS):
        out = out + gates[:, e:e + 1] * swiglu(h, wg[e], wu[e], wd[e]).astype(jnp.float32)
    return out.astype(h.dtype)


def channel_mixer(x, l, norm_g, w_gate_d, w_up_d, w_down_d, router, w_gate_e, w_up_e, w_down_e):
    n, t, d = x.shape
    h = rmsnorm(x, norm_g).reshape(n * t, d)
    i = l // 2
    if l % 2 == 0:
        y = swiglu(h, w_gate_d[i], w_up_d[i], w_down_d[i])
    else:
        y = moe(h, router[i], w_gate_e[i], w_up_e[i], w_down_e[i])
    return y.reshape(n, t, d)


def setup_inputs(seed: int = 0) -> dict:
    key = jax.random.key(seed)
    ks = jax.random.split(key, 24)
    f32 = jnp.float32

    def nrm(k, shape, scale=1.0):
        return jax.random.normal(k, shape, f32) * scale

    n_pages = PAST_LEN // PAGE_SIZE
    n_pool = (DEC_BATCH * n_pages * 5) // 4
    page_table = jax.random.permutation(ks[5], n_pool)[:DEC_BATCH * n_pages].reshape(DEC_BATCH, n_pages).astype(jnp.int32)
    return {
        'x_prompt': nrm(ks[0], (BATCH, SEQ, D_MODEL)),
        'x_sample': nrm(ks[1], (DEC_BATCH, DEC_SEQ, D_MODEL)),
        'cache_k': nrm(ks[2], (DEPTH, n_pool, PAGE_SIZE, MOBA_HEADS, MOBA_DH)),
        'cache_v': nrm(ks[3], (DEPTH, n_pool, PAGE_SIZE, MOBA_HEADS, MOBA_DH)),
        'state_gla': nrm(ks[4], (DEPTH, DEC_BATCH, GLA_HEADS, GLA_DK, GLA_DV)),
        'page_table': page_table,
        'norm1_g': 1.0 + nrm(ks[6], (DEPTH, D_MODEL), 0.01),
        'w_in': nrm(ks[7], (DEPTH, D_MODEL, IN_COLS), D_MODEL ** -0.5),
        'gla_w_a2': nrm(ks[8], (DEPTH, GLA_RANK, GLA_QK_W), GLA_RANK ** -0.5),
        'gla_b_a': nrm(ks[9], (DEPTH, GLA_QK_W), 0.1),
        'gla_norm_g': 1.0 + nrm(ks[10], (DEPTH, GLA_DV), 0.01),
        'w_gla_br': nrm(ks[11], (DEPTH, GLA_V_W, D_MODEL), GLA_V_W ** -0.5),
        'w_moba_br': nrm(ks[12], (DEPTH, MOBA_W, D_MODEL), MOBA_W ** -0.5),
        'w_out': nrm(ks[13], (DEPTH, D_MODEL, D_MODEL), D_MODEL ** -0.5),
        'norm2_g': 1.0 + nrm(ks[14], (DEPTH, D_MODEL), 0.01),
        'w_gate_d': nrm(ks[15], (N_DENSE, D_MODEL, D_FF), D_MODEL ** -0.5),
        'w_up_d': nrm(ks[16], (N_DENSE, D_MODEL, D_FF), D_MODEL ** -0.5),
        'w_down_d': nrm(ks[17], (N_DENSE, D_FF, D_MODEL), D_FF ** -0.5),
        'router': nrm(ks[18], (N_MOE, D_MODEL, N_EXPERTS), D_MODEL ** -0.5),
        'w_gate_e': nrm(ks[19], (N_MOE, N_EXPERTS, D_MODEL, D_FF_EXPERT), D_MODEL ** -0.5),
        'w_up_e': nrm(ks[20], (N_MOE, N_EXPERTS, D_MODEL, D_FF_EXPERT), D_MODEL ** -0.5),
        'w_down_e': nrm(ks[21], (N_MOE, N_EXPERTS, D_FF_EXPERT, D_MODEL), D_FF_EXPERT ** -0.5),
        'final_g': 1.0 + nrm(ks[22], (D_MODEL,), 0.01),
    }


def reference(x_prompt, x_sample, cache_k, cache_v, state_gla, page_table, norm1_g, w_in, gla_w_a2, gla_b_a,
              gla_norm_g, w_gla_br, w_moba_br, w_out, norm2_g, w_gate_d, w_up_d, w_down_d, router,
              w_gate_e, w_up_e, w_down_e, final_g):
    xp = x_prompt
    xs = x_sample
    k_rows_p, v_rows_p, gla_p = [], [], []
    k_rows_s, v_rows_s, gla_s = [], [], []
    for l in range(DEPTH):
        xn = rmsnorm(xp, norm1_g[l])
        qg, kg, vg, la, r, qm, km, vm, ga, gb = mixer_inputs(xn, w_in[l], gla_w_a2[l], gla_b_a[l])
        og, s_fin = gla_prompt(qg, kg, vg, la)
        og = gla_post(og, r, gla_norm_g[l])
        om = moba_prompt(qm, km, vm)
        xp = xp + mix_out(og, om, ga, gb, w_gla_br[l], w_moba_br[l], w_out[l]).astype(xp.dtype)
        xp = xp + channel_mixer(xp, l, norm2_g[l], w_gate_d, w_up_d, w_down_d, router, w_gate_e, w_up_e, w_down_e).astype(xp.dtype)
        k_rows_p.append(km)
        v_rows_p.append(vm)
        gla_p.append(s_fin.astype(state_gla.dtype))
        xn = rmsnorm(xs, norm1_g[l])
        qg, kg, vg, la, r, qm, km, vm, ga, gb = mixer_inputs(xn, w_in[l], gla_w_a2[l], gla_b_a[l])
        s_new, og = gla_chunk_step(state_gla[l].astype(jnp.float32), (qg, kg, vg, la))
        og = gla_post(og, r, gla_norm_g[l])
        om = moba_sample(qm, km, vm, cache_k[l], cache_v[l], page_table)
        xs = xs + mix_out(og, om, ga, gb, w_gla_br[l], w_moba_br[l], w_out[l]).astype(xs.dtype)
        xs = xs + channel_mixer(xs, l, norm2_g[l], w_gate_d, w_up_d, w_down_d, router, w_gate_e, w_up_e, w_down_e).astype(xs.dtype)
        k_rows_s.append(km)
        v_rows_s.append(vm)
        gla_s.append(s_new.astype(state_gla.dtype))
    y_prompt = rmsnorm(xp, final_g)
    y_sample = rmsnorm(xs, final_g)
    return (y_prompt, y_sample, jnp.stack(k_rows_p), jnp.stack(v_rows_p), jnp.stack(gla_p),
            jnp.stack(k_rows_s), jnp.stack(v_rows_s), jnp.stack(gla_s))
```

```python
import functools

import jax
import jax.numpy as jnp
from jax import lax
from jax.experimental import pallas as pl
from jax.experimental.pallas import tpu as pltpu

F32 = jnp.float32
MXU_DTYPE = jnp.bfloat16

GLA_HEADS = 4
GLA_DK = 256
GLA_DV = 512
GLA_RANK = 16
GLA_TAU = 16.0
MOBA_HEADS = 16
MOBA_DH = 128
MOBA_BLOCK = 256
MOBA_TOPK = 3
PAGE_SIZE = 128
N_EXPERTS = 8
RMS_EPS = 1e-6

LANES = 128
VMEM_LIMIT_BYTES = 56 * 1024 * 1024

H_QG, H_KG, H_VG, H_RG, H_QM, H_KM, H_VM, H_GA, H_GB = (
    0, 1024, 2048, 4096, 6144, 8192, 10240, 12288, 14336)
H_COLS = 16384

GLA_CHUNK = 128
ROW_TILE = 512
NEG_INF = float("-inf")


def _params(*sem):
    return pltpu.CompilerParams(dimension_semantics=sem, vmem_limit_bytes=VMEM_LIMIT_BYTES)


def _mm(a, b):
    return jnp.dot(a.astype(MXU_DTYPE), b.astype(MXU_DTYPE), preferred_element_type=F32)


def _mm_nt(a, b):
    return lax.dot_general(a.astype(MXU_DTYPE), b.astype(MXU_DTYPE), (((1,), (1,)), ((), ())),
                           preferred_element_type=F32)


def _split3(a):
    hi = a.astype(MXU_DTYPE)
    r1 = a - hi.astype(F32)
    mid = r1.astype(MXU_DTYPE)
    lo = (r1 - mid.astype(F32)).astype(MXU_DTYPE)
    return hi, mid, lo


def _mm_f32(a, b, nt=False):
    f = _mm_nt if nt else _mm
    a0, a1, a2 = _split3(a)
    b0, b1, b2 = _split3(b)
    out = f(a0, b0)
    out = out + (f(a0, b1) + f(a1, b0))
    out = out + (f(a0, b2) + f(a2, b0) + f(a1, b1))
    return out


def _mm_exact_lhs(a01, b):
    b0, b1, b2 = _split3(b)
    a = a01.astype(MXU_DTYPE)
    return _mm(a, b0) + _mm(a, b1) + _mm(a, b2)


def _iota(shape, dim):
    return lax.broadcasted_iota(jnp.int32, shape, dim)


def _inproj_kernel(x_ref, g_ref, w_ref, wlr_ref, wa2_ref, ba_ref, h_ref, la_ref, xn_ref):
    @pl.when(pl.program_id(1) == 0)
    def _():
        x = x_ref[...]
        ms = jnp.mean(x * x, axis=-1, keepdims=True)
        xn_ref[...] = ((x * lax.rsqrt(ms + RMS_EPS)) * g_ref[...]).astype(xn_ref.dtype)
        a_lr = jnp.dot(xn_ref[...], wlr_ref[...], preferred_element_type=F32)
        z = _mm(a_lr, wa2_ref[...]) + ba_ref[...]
        log_sig = jnp.minimum(z, 0.0) - jnp.log1p(jnp.exp(-jnp.abs(z)))
        la_ref[...] = log_sig * (1.0 / GLA_TAU)

    h_ref[...] = jnp.dot(xn_ref[...], w_ref[...], preferred_element_type=F32)


def _inproj(x, g, w_main, w_lr, w_a2, b_a, *, tn=1024):
    m, d = x.shape
    tm = ROW_TILE
    n_qk = w_a2.shape[1]
    return pl.pallas_call(
        _inproj_kernel,
        grid=(m // tm, H_COLS // tn),
        in_specs=[
            pl.BlockSpec((tm, d), lambda i, j: (i, 0)),
            pl.BlockSpec((1, d), lambda i, j: (0, 0)),
            pl.BlockSpec((d, tn), lambda i, j: (0, j)),
            pl.BlockSpec((d, LANES), lambda i, j: (0, 0)),
            pl.BlockSpec((LANES, n_qk), lambda i, j: (0, 0)),
            pl.BlockSpec((1, n_qk), lambda i, j: (0, 0)),
        ],
        out_specs=[
            pl.BlockSpec((tm, tn), lambda i, j: (i, j)),
            pl.BlockSpec((tm, n_qk), lambda i, j: (i, 0)),
        ],
        out_shape=[jax.ShapeDtypeStruct((m, H_COLS), F32), jax.ShapeDtypeStruct((m, n_qk), F32)],
        scratch_shapes=[pltpu.VMEM((tm, d), MXU_DTYPE)],
        compiler_params=_params("parallel", "arbitrary"),
        name="inproj",
    )(x, g, w_main, w_lr, w_a2, b_a)


def _gla_chunk(q, k, v, la, s):
    n = q.shape[0]
    row, col = _iota((n, n), 0), _iota((n, n), 1)
    causal = row >= col
    b = _mm_exact_lhs(jnp.where(causal, 1.0, 0.0), la)
    b_t = b.T
    b_last = b_t[:, n - 1:n]
    qd = q * jnp.exp(b) * (GLA_DK ** -0.5)
    k_inv = k * jnp.exp(-b)
    kd_t = k.T * jnp.exp(b_last - b_t)
    attn = jnp.where(causal, _mm_nt(qd, k_inv), 0.0)
    o = _mm(qd, s) + _mm(attn, v)
    s_new = jnp.exp(b_last) * s + _mm(kd_t, v)
    return o, s_new


def _gla_post(o, r, gain):
    ms = jnp.mean(o * o, axis=-1, keepdims=True)
    return (o * lax.rsqrt(ms + RMS_EPS) * gain) * (r * jax.nn.sigmoid(r))


def _gla_prompt_kernel(q_ref, k_ref, v_ref, r_ref, la_ref, gain_ref, og_ref, s_ref):
    @pl.when(pl.program_id(2) == 0)
    def _():
        s_ref[...] = jnp.zeros_like(s_ref)

    o, s_new = _gla_chunk(q_ref[...], k_ref[...], v_ref[...], la_ref[...], s_ref[...])
    s_ref[...] = s_new
    og_ref[...] = _gla_post(o, r_ref[...], gain_ref[...]).astype(og_ref.dtype)


def _gla_prompt(h, log_a, gain, batch, seq):
    m = h.shape[0]
    nc = seq // GLA_CHUNK
    cb = GLA_CHUNK
    row = lambda b, hd, c: b * nc + c
    return pl.pallas_call(
        _gla_prompt_kernel,
        grid=(batch, GLA_HEADS, nc),
        in_specs=[
            pl.BlockSpec((cb, GLA_DK), lambda b, hd, c: (row(b, hd, c), H_QG // GLA_DK + hd)),
            pl.BlockSpec((cb, GLA_DK), lambda b, hd, c: (row(b, hd, c), H_KG // GLA_DK + hd)),
            pl.BlockSpec((cb, GLA_DV), lambda b, hd, c: (row(b, hd, c), H_VG // GLA_DV + hd)),
            pl.BlockSpec((cb, GLA_DV), lambda b, hd, c: (row(b, hd, c), H_RG // GLA_DV + hd)),
            pl.BlockSpec((cb, GLA_DK), lambda b, hd, c: (row(b, hd, c), hd)),
            pl.BlockSpec((1, GLA_DV), lambda b, hd, c: (0, 0)),
        ],
        out_specs=[
            pl.BlockSpec((cb, GLA_DV), lambda b, hd, c: (row(b, hd, c), hd)),
            pl.BlockSpec((None, None, GLA_DK, GLA_DV), lambda b, hd, c: (b, hd, 0, 0)),
        ],
        out_shape=[
            jax.ShapeDtypeStruct((m, GLA_HEADS * GLA_DV), MXU_DTYPE),
            jax.ShapeDtypeStruct((batch, GLA_HEADS, GLA_DK, GLA_DV), F32),
        ],
        compiler_params=_params("parallel", "parallel", "arbitrary"),
        name="gla_prompt",
    )(h, h, h, h, log_a, gain)


def _gla_sample_kernel(q_ref, k_ref, v_ref, r_ref, la_ref, gain_ref, s0_ref, og_ref, s_ref):
    t = q_ref.shape[0]

    def pad(a):
        return jnp.concatenate([a, jnp.zeros((GLA_CHUNK - t, a.shape[1]), a.dtype)], axis=0)

    o, s_new = _gla_chunk(pad(q_ref[...]), pad(k_ref[...]), pad(v_ref[...]), pad(la_ref[...]), s0_ref[...])
    s_ref[...] = s_new
    og_ref[...] = _gla_post(o[:t], r_ref[...], gain_ref[...]).astype(og_ref.dtype)


def _gla_sample(h, log_a, gain, state, layer, n_seq, t):
    m = h.shape[0]
    return pl.pallas_call(
        _gla_sample_kernel,
        grid=(n_seq, GLA_HEADS),
        in_specs=[
            pl.BlockSpec((t, GLA_DK), lambda s, hd: (s, H_QG // GLA_DK + hd)),
            pl.BlockSpec((t, GLA_DK), lambda s, hd: (s, H_KG // GLA_DK + hd)),
            pl.BlockSpec((t, GLA_DV), lambda s, hd: (s, H_VG // GLA_DV + hd)),
            pl.BlockSpec((t, GLA_DV), lambda s, hd: (s, H_RG // GLA_DV + hd)),
            pl.BlockSpec((t, GLA_DK), lambda s, hd: (s, hd)),
            pl.BlockSpec((1, GLA_DV), lambda s, hd: (0, 0)),
            pl.BlockSpec((None, None, None, GLA_DK, GLA_DV), lambda s, hd: (layer, s, hd, 0, 0)),
        ],
        out_specs=[
            pl.BlockSpec((t, GLA_DV), lambda s, hd: (s, hd)),
            pl.BlockSpec((None, None, GLA_DK, GLA_DV), lambda s, hd: (s, hd, 0, 0)),
        ],
        out_shape=[
            jax.ShapeDtypeStruct((m, GLA_HEADS * GLA_DV), F32),
            jax.ShapeDtypeStruct((n_seq, GLA_HEADS, GLA_DK, GLA_DV), F32),
        ],
        compiler_params=_params("parallel", "parallel"),
        name="gla_sample",
    )(h, h, h, h, log_a, gain, state)


def _select_topk(g):
    nb = g.shape[0]
    n_idx = _iota(g.shape, 0)
    rank = jnp.zeros(g.shape, F32)
    for mth in range(nb):
        gm = g[mth:mth + 1, :]
        ahead = jnp.where(gm > g, 1.0, jnp.where(gm == g, jnp.where(n_idx > mth, 1.0, 0.0), 0.0))
        rank = rank + ahead
    return jnp.where(rank < float(MOBA_TOPK), jnp.where(g > NEG_INF, 1.0, 0.0), 0.0)


def _moba_prompt_kernel(q_ref, k_ref, v_ref, o_ref, kb_ref, vt_ref, kmean_ref, sel_ref, m_ref, l_ref,
                        acc_ref):
    i = pl.program_id(2)
    nb = kb_ref.shape[0]
    blk = MOBA_BLOCK
    scale = MOBA_DH ** -0.5

    @pl.when(i == 0)
    def _():
        for j in range(nb):
            kj = k_ref[j * blk:(j + 1) * blk, :]
            kb_ref[j] = kj.astype(kb_ref.dtype)
            kmean_ref[j:j + 1, :] = jnp.mean(kj, axis=0, keepdims=True)
            vt_ref[j] = v_ref[j * blk:(j + 1) * blk, :].T.astype(vt_ref.dtype)

    q_t = q_ref[...].T
    q_tb = q_t.astype(MXU_DTYPE)
    gate = _mm_f32(kmean_ref[...], q_t)
    gate = jnp.where(_iota(gate.shape, 0) < i, gate, NEG_INF)
    sel_ref[...] = _select_topk(gate)

    def tile(kb, vt, bias):
        s = jnp.dot(kb, q_tb, preferred_element_type=F32) * scale + bias
        m_prev = m_ref[...]
        m_new = jnp.maximum(m_prev, jnp.max(s, axis=0, keepdims=True))
        alpha = jnp.exp(m_prev - m_new)
        p = jnp.exp(s - m_new)
        l_ref[...] = alpha * l_ref[...] + jnp.sum(p, axis=0, keepdims=True)
        acc_ref[...] = alpha * acc_ref[...] + jnp.dot(vt, p.astype(MXU_DTYPE), preferred_element_type=F32)
        m_ref[...] = m_new

    m_ref[...] = jnp.full(m_ref.shape, NEG_INF, F32)
    l_ref[...] = jnp.zeros(l_ref.shape, F32)
    acc_ref[...] = jnp.zeros(acc_ref.shape, F32)
    tile(kb_ref[i], vt_ref[i], jnp.where(_iota((blk, blk), 0) <= _iota((blk, blk), 1), 0.0, NEG_INF))

    def past(j, carry):
        tile(kb_ref[j], vt_ref[j], jnp.where(sel_ref[pl.ds(j, 1), :] > 0.0, 0.0, NEG_INF))
        return carry

    lax.fori_loop(0, i, past, 0)
    o_ref[...] = (acc_ref[...] / l_ref[...]).T.astype(o_ref.dtype)


def _moba_prompt(h, batch, seq):
    m = h.shape[0]
    nb = seq // MOBA_BLOCK
    cq, ck, cv = H_QM // MOBA_DH, H_KM // MOBA_DH, H_VM // MOBA_DH
    return pl.pallas_call(
        _moba_prompt_kernel,
        grid=(batch, MOBA_HEADS, nb),
        in_specs=[
            pl.BlockSpec((MOBA_BLOCK, MOBA_DH), lambda b, hd, i: (b * nb + i, cq + hd)),
            pl.BlockSpec((seq, MOBA_DH), lambda b, hd, i: (b, ck + hd)),
            pl.BlockSpec((seq, MOBA_DH), lambda b, hd, i: (b, cv + hd)),
        ],
        out_specs=pl.BlockSpec((MOBA_BLOCK, MOBA_DH), lambda b, hd, i: (b * nb + i, hd)),
        out_shape=jax.ShapeDtypeStruct((m, MOBA_HEADS * MOBA_DH), MXU_DTYPE),
        scratch_shapes=[
            pltpu.VMEM((nb, MOBA_BLOCK, MOBA_DH), MXU_DTYPE),
            pltpu.VMEM((nb, MOBA_DH, MOBA_BLOCK), MXU_DTYPE),
            pltpu.VMEM((nb, MOBA_DH), F32),
            pltpu.VMEM((nb, MOBA_BLOCK), F32),
            pltpu.VMEM((1, MOBA_BLOCK), F32),
            pltpu.VMEM((1, MOBA_BLOCK), F32),
            pltpu.VMEM((MOBA_DH, MOBA_BLOCK), F32),
        ],
        compiler_params=_params("parallel", "parallel", "arbitrary"),
        name="moba_prompt",
    )(h, h, h)


def _moba_sample_kernel(pt_ref, q_ref, kn_ref, vn_ref, ka_ref, kb_ref, va_ref, vb_ref, o_ref,
                        qbd_ref, st_ref, p_ref, pown_ref, ksum_ref, acc_ref):
    del pt_ref
    p = pl.program_id(1)
    nblk = ksum_ref.shape[0]
    t = q_ref.shape[0]
    hq = MOBA_HEADS * t
    d = q_ref.shape[1]
    scale = MOBA_DH ** -0.5

    def head_rows(page_ref, hd):
        return page_ref[pl.ds(hd, PAGE_SIZE, stride=MOBA_HEADS), :]

    @pl.when(p == 0)
    def _():
        q_rep = jnp.concatenate([q_ref[...]] * MOBA_HEADS, axis=0)
        same_head = (_iota((hq, d), 0) // t) == (_iota((hq, d), 1) // MOBA_DH)
        qbd_ref[...] = jnp.where(same_head, q_rep, 0.0)

    @pl.when(p < nblk)
    def _():
        for half, ref in enumerate((ka_ref, kb_ref)):
            kp = jnp.concatenate([head_rows(ref, hd) for hd in range(MOBA_HEADS)], axis=1)
            st = _mm_nt(kp, qbd_ref[...]) * scale
            st_ref[pl.ds(pl.multiple_of(p * MOBA_BLOCK + half * PAGE_SIZE, PAGE_SIZE), PAGE_SIZE), :] = st
            part = jnp.sum(kp, axis=0, keepdims=True)
            if half == 0:
                ksum_ref[pl.ds(p, 1), :] = part
            else:
                ksum_ref[pl.ds(p, 1), :] = ksum_ref[pl.ds(p, 1), :] + part

    @pl.when(p == nblk - 1)
    def _():
        kmean = ksum_ref[...] * (1.0 / MOBA_BLOCK)
        gate = _mm_f32(kmean, qbd_ref[...], nt=True)
        sel = _select_topk(gate)
        s_own = _mm_nt(kn_ref[...], qbd_ref[...]) * scale
        s_own = jnp.where(_iota((t, hq), 0) <= _iota((t, hq), 1) % t, s_own, NEG_INF)
        m = jnp.max(s_own, axis=0, keepdims=True)
        nb_static = nblk
        for j in range(nb_static):
            sj = jnp.where(sel[j:j + 1, :] > 0.0, st_ref[j * MOBA_BLOCK:(j + 1) * MOBA_BLOCK, :], NEG_INF)
            m = jnp.maximum(m, jnp.max(sj, axis=0, keepdims=True))
        e_own = jnp.exp(s_own - m)
        l = jnp.sum(e_own, axis=0, keepdims=True)
        for j in range(nb_static):
            sj = jnp.where(sel[j:j + 1, :] > 0.0, st_ref[j * MOBA_BLOCK:(j + 1) * MOBA_BLOCK, :], NEG_INF)
            ej = jnp.exp(sj - m)
            p_ref[j * MOBA_BLOCK:(j + 1) * MOBA_BLOCK, :] = ej
            l = l + jnp.sum(ej, axis=0, keepdims=True)
        inv = 1.0 / l
        for j in range(nb_static):
            p_ref[j * MOBA_BLOCK:(j + 1) * MOBA_BLOCK, :] = p_ref[j * MOBA_BLOCK:(j + 1) * MOBA_BLOCK, :] * inv
        pown_ref[...] = jnp.concatenate([e_own * inv, jnp.zeros((PAGE_SIZE - t, hq), F32)], axis=0)
        acc_ref[...] = jnp.zeros_like(acc_ref)

    def accumulate(prob, head_vals):
        prob_t = prob.T
        for hd in range(MOBA_HEADS):
            lo, hi = hd * MOBA_DH, (hd + 1) * MOBA_DH
            acc_ref[:, lo:hi] = acc_ref[:, lo:hi] + _mm(prob_t[hd * t:(hd + 1) * t, :], head_vals(hd))

    @pl.when(p >= nblk)
    def _():
        base = (p - nblk) * MOBA_BLOCK
        for half, ref in enumerate((va_ref, vb_ref)):
            off = pl.multiple_of(base + half * PAGE_SIZE, PAGE_SIZE)
            accumulate(p_ref[pl.ds(off, PAGE_SIZE), :], functools.partial(head_rows, ref))

    @pl.when(p == 2 * nblk - 1)
    def _():
        v_own = jnp.concatenate([vn_ref[...], jnp.zeros((PAGE_SIZE - t, d), F32)], axis=0)
        accumulate(pown_ref[...], lambda hd: v_own[:, hd * MOBA_DH:(hd + 1) * MOBA_DH])
        o_ref[...] = acc_ref[...].astype(o_ref.dtype)


def _moba_sample(h, cache_k, cache_v, page_table, layer, n_seq, t):
    m = h.shape[0]
    d = MOBA_HEADS * MOBA_DH
    n_pages = page_table.shape[1]
    nblk = n_pages * PAGE_SIZE // MOBA_BLOCK
    ppb = MOBA_BLOCK // PAGE_SIZE
    assert ppb == 2
    hq = MOBA_HEADS * t
    assert hq == LANES
    ck = cache_k.reshape(cache_k.shape[0], cache_k.shape[1], PAGE_SIZE * MOBA_HEADS, MOBA_DH)
    cv = cache_v.reshape(cache_v.shape[0], cache_v.shape[1], PAGE_SIZE * MOBA_HEADS, MOBA_DH)

    def kpage(half):
        return lambda s, p, pt: (layer, pt[s, ppb * jnp.minimum(p, nblk - 1) + half], 0, 0)

    def vpage(half):
        return lambda s, p, pt: (layer, pt[s, ppb * jnp.maximum(p - nblk, 0) + half], 0, 0)

    page_spec = lambda f: pl.BlockSpec((None, None, PAGE_SIZE * MOBA_HEADS, MOBA_DH), f)
    grid_spec = pltpu.PrefetchScalarGridSpec(
        num_scalar_prefetch=1,
        grid=(n_seq, 2 * nblk),
        in_specs=[
            pl.BlockSpec((t, d), lambda s, p, pt: (s, H_QM // d)),
            pl.BlockSpec((t, d), lambda s, p, pt: (s, H_KM // d)),
            pl.BlockSpec((t, d), lambda s, p, pt: (s, H_VM // d)),
            page_spec(kpage(0)), page_spec(kpage(1)), page_spec(vpage(0)), page_spec(vpage(1)),
        ],
        out_specs=pl.BlockSpec((t, d), lambda s, p, pt: (s, 0)),
        scratch_shapes=[
            pltpu.VMEM((hq, d), F32),
            pltpu.VMEM((n_pages * PAGE_SIZE, hq), F32),
            pltpu.VMEM((n_pages * PAGE_SIZE, hq), F32),
            pltpu.VMEM((PAGE_SIZE, hq), F32),
            pltpu.VMEM((nblk, d), F32),
            pltpu.VMEM((t, d), F32),
        ],
    )
    return pl.pallas_call(
        _moba_sample_kernel,
        grid_spec=grid_spec,
        out_shape=jax.ShapeDtypeStruct((m, d), F32),
        compiler_params=_params("parallel", "arbitrary"),
        name="moba_sample",
    )(page_table, h, h, h, ck, ck, cv, cv)


def _merge_kernel(og_ref, om_ref, ga_ref, gb_ref, wa_ref, wb_ref, m_ref):
    ya = _mm(og_ref[...], wa_ref[...])
    yb = _mm(om_ref[...], wb_ref[...])
    m_ref[...] = (jax.nn.sigmoid(ga_ref[...]) * ya + jax.nn.sigmoid(gb_ref[...]) * yb).astype(m_ref.dtype)


def _merge(og, om, h, w_a, w_b, *, tn=512):
    m, d = og.shape
    tm = ROW_TILE
    return pl.pallas_call(
        _merge_kernel,
        grid=(m // tm, d // tn),
        in_specs=[
            pl.BlockSpec((tm, d), lambda i, j: (i, 0)),
            pl.BlockSpec((tm, d), lambda i, j: (i, 0)),
            pl.BlockSpec((tm, tn), lambda i, j: (i, H_GA // tn + j)),
            pl.BlockSpec((tm, tn), lambda i, j: (i, H_GB // tn + j)),
            pl.BlockSpec((d, tn), lambda i, j: (0, j)),
            pl.BlockSpec((d, tn), lambda i, j: (0, j)),
        ],
        out_specs=pl.BlockSpec((tm, tn), lambda i, j: (i, j)),
        out_shape=jax.ShapeDtypeStruct((m, d), MXU_DTYPE),
        compiler_params=_params("parallel", "arbitrary"),
        name="merge",
    )(og, om, h, h, w_a, w_b)


def _outproj_kernel(x_ref, m_ref, wo_ref, g_ref, router_ref, xo_ref, hn_ref, logit_ref):
    x = x_ref[...] + jnp.dot(m_ref[...], wo_ref[...], preferred_element_type=F32)
    xo_ref[...] = x
    ms = jnp.mean(x * x, axis=-1, keepdims=True)
    hn = (x * lax.rsqrt(ms + RMS_EPS)) * g_ref[...]
    hn_ref[...] = hn.astype(hn_ref.dtype)
    logit_ref[...] = _mm_f32(hn, router_ref[...])


def _outproj(x, mix, w_o, g2, router_pad):
    m, d = x.shape
    tm = ROW_TILE
    return pl.pallas_call(
        _outproj_kernel,
        grid=(m // tm,),
        in_specs=[
            pl.BlockSpec((tm, d), lambda i: (i, 0)),
            pl.BlockSpec((tm, d), lambda i: (i, 0)),
            pl.BlockSpec((d, d), lambda i: (0, 0)),
            pl.BlockSpec((1, d), lambda i: (0, 0)),
            pl.BlockSpec((d, LANES), lambda i: (0, 0)),
        ],
        out_specs=[
            pl.BlockSpec((tm, d), lambda i: (i, 0)),
            pl.BlockSpec((tm, d), lambda i: (i, 0)),
            pl.BlockSpec((tm, LANES), lambda i: (i, 0)),
        ],
        out_shape=[
            jax.ShapeDtypeStruct((m, d), F32),
            jax.ShapeDtypeStruct((m, d), MXU_DTYPE),
            jax.ShapeDtypeStruct((m, LANES), F32),
        ],
        compiler_params=_params("parallel"),
        name="outproj",
    )(x, mix, w_o, g2, router_pad)


def _ffn_kernel(x_ref, hn_ref, wg_ref, wu_ref, wd_ref, o_ref):
    @pl.when(pl.program_id(1) == 0)
    def _():
        o_ref[...] = x_ref[...]

    hn = hn_ref[...]
    gate = jnp.dot(hn, wg_ref[...], preferred_element_type=F32)
    up = jnp.dot(hn, wu_ref[...], preferred_element_type=F32)
    act = (gate * jax.nn.sigmoid(gate)) * up
    o_ref[...] += _mm(act, wd_ref[...])


def _ffn(x, hn, wg, wu, wd, *, tf=512):
    m, d = x.shape
    f = wg.shape[1]
    tm = ROW_TILE
    return pl.pallas_call(
        _ffn_kernel,
        grid=(m // tm, f // tf),
        in_specs=[
            pl.BlockSpec((tm, d), lambda i, j: (i, 0)),
            pl.BlockSpec((tm, d), lambda i, j: (i, 0)),
            pl.BlockSpec((d, tf), lambda i, j: (0, j)),
            pl.BlockSpec((d, tf), lambda i, j: (0, j)),
            pl.BlockSpec((tf, d), lambda i, j: (j, 0)),
        ],
        out_specs=pl.BlockSpec((tm, d), lambda i, j: (i, 0)),
        out_shape=jax.ShapeDtypeStruct((m, d), F32),
        compiler_params=_params("parallel", "arbitrary"),
        name="ffn",
    )(x, hn, wg, wu, wd)


def _moe_kernel(x_ref, hn_ref, logit_ref, wg_ref, wu_ref, wd_ref, o_ref, gates_ref, acc_ref):
    e, j = pl.program_id(1), pl.program_id(2)
    nj = pl.num_programs(2)

    @pl.when((e == 0) & (j == 0))
    def _():
        o_ref[...] = x_ref[...]
        lg = logit_ref[...]
        lane = _iota(lg.shape, 1)
        lg = jnp.where(lane < N_EXPERTS, lg, NEG_INF)
        m1 = jnp.max(lg, axis=-1, keepdims=True)
        i1 = jnp.min(jnp.where(lg == m1, lane, LANES), axis=-1, keepdims=True)
        lg2 = jnp.where(lane == i1, NEG_INF, lg)
        m2 = jnp.max(lg2, axis=-1, keepdims=True)
        i2 = jnp.min(jnp.where(lg2 == m2, lane, LANES), axis=-1, keepdims=True)
        e2 = jnp.exp(m2 - m1)
        w1 = 1.0 / (1.0 + e2)
        w2 = e2 / (1.0 + e2)
        gates_ref[...] = jnp.where(lane == i1, w1, 0.0) + jnp.where(lane == i2, w2, 0.0)

    hn = hn_ref[...]
    gate = jnp.dot(hn, wg_ref[...], preferred_element_type=F32)
    up = jnp.dot(hn, wu_ref[...], preferred_element_type=F32)
    act = (gate * jax.nn.sigmoid(gate)) * up
    y = _mm(act, wd_ref[...])

    @pl.when(j == 0)
    def _():
        acc_ref[...] = y

    @pl.when(j > 0)
    def _():
        acc_ref[...] += y

    @pl.when(j == nj - 1)
    def _():
        gates = gates_ref[...]
        ge = jnp.sum(jnp.where(_iota(gates.shape, 1) == e, gates, 0.0), axis=-1, keepdims=True)
        o_ref[...] += ge * acc_ref[...]


def _moe(x, hn, logits, wg, wu, wd, *, tf=768):
    m, d = x.shape
    ne, _, f = wg.shape
    tm = ROW_TILE
    return pl.pallas_call(
        _moe_kernel,
        grid=(m // tm, ne, f // tf),
        in_specs=[
            pl.BlockSpec((tm, d), lambda i, e, j: (i, 0)),
            pl.BlockSpec((tm, d), lambda i, e, j: (i, 0)),
            pl.BlockSpec((tm, LANES), lambda i, e, j: (i, 0)),
            pl.BlockSpec((None, d, tf), lambda i, e, j: (e, 0, j)),
            pl.BlockSpec((None, d, tf), lambda i, e, j: (e, 0, j)),
            pl.BlockSpec((None, tf, d), lambda i, e, j: (e, j, 0)),
        ],
        out_specs=pl.BlockSpec((tm, d), lambda i, e, j: (i, 0)),
        out_shape=jax.ShapeDtypeStruct((m, d), F32),
        scratch_shapes=[pltpu.VMEM((tm, LANES), F32), pltpu.VMEM((tm, d), F32)],
        compiler_params=_params("parallel", "arbitrary", "arbitrary"),
        name="moe",
    )(x, hn, logits, wg, wu, wd)


def _norm_kernel(x_ref, g_ref, o_ref):
    x = x_ref[...]
    ms = jnp.mean(x * x, axis=-1, keepdims=True)
    o_ref[...] = (x * lax.rsqrt(ms + RMS_EPS)) * g_ref[...]


def _final_norm(x, g):
    m, d = x.shape
    tm = ROW_TILE
    return pl.pallas_call(
        _norm_kernel,
        grid=(m // tm,),
        in_specs=[pl.BlockSpec((tm, d), lambda i: (i, 0)), pl.BlockSpec((1, d), lambda i: (0, 0))],
        out_specs=pl.BlockSpec((tm, d), lambda i: (i, 0)),
        out_shape=jax.ShapeDtypeStruct((m, d), F32),
        compiler_params=_params("parallel"),
        name="final_norm",
    )(x, g)


def _pad_to(a, axis, size):
    pad = [(0, 0)] * a.ndim
    pad[axis] = (0, size - a.shape[axis])
    return jnp.pad(a, pad)


def kernel(x_prompt, x_sample, cache_k, cache_v, state_gla, page_table, norm1_g, w_in, gla_w_a2, gla_b_a,
           gla_norm_g, w_gla_br, w_moba_br, w_out, norm2_g, w_gate_d, w_up_d, w_down_d, router,
           w_gate_e, w_up_e, w_down_e, final_g):
    batch, seq, d = x_prompt.shape
    n_seq, t_new, _ = x_sample.shape
    depth = w_in.shape[0]
    lr_lo = 2 * GLA_HEADS * GLA_DK + GLA_HEADS * GLA_DV
    lr_hi = lr_lo + GLA_RANK
    f_pad = -(-w_gate_e.shape[-1] // 768) * 768

    xp = x_prompt.reshape(batch * seq, d)
    xs = x_sample.reshape(n_seq * t_new, d)
    k_p, v_p, s_p, k_s, v_s, s_s = [], [], [], [], [], []
    for l in range(depth):
        bf = lambda a: a.astype(MXU_DTYPE)
        w_main = bf(jnp.concatenate([w_in[l, :, :lr_lo], w_in[l, :, lr_hi:]], axis=1))
        w_lr = bf(_pad_to(w_in[l, :, lr_lo:lr_hi], 1, LANES))
        w_a2 = bf(_pad_to(gla_w_a2[l], 0, LANES))
        b_a = gla_b_a[l][None, :]
        g1 = norm1_g[l][None, :]
        g2 = norm2_g[l][None, :]
        gain = gla_norm_g[l][None, :]
        w_a, w_b, w_o = bf(w_gla_br[l]), bf(w_moba_br[l]), bf(w_out[l])
        is_moe = l % 2 == 1
        li = l // 2
        if is_moe:
            router_pad = _pad_to(router[li], 1, LANES)
            wg = bf(_pad_to(w_gate_e[li], 2, f_pad))
            wu = bf(_pad_to(w_up_e[li], 2, f_pad))
            wd = bf(_pad_to(w_down_e[li], 1, f_pad))
        else:
            router_pad = jnp.zeros((d, LANES), F32)
            wg, wu, wd = bf(w_gate_d[li]), bf(w_up_d[li]), bf(w_down_d[li])

        def mixer(x, mix):
            x_mid, hn, logits = _outproj(x, mix, w_o, g2, router_pad)
            if is_moe:
                return _moe(x_mid, hn, logits, wg, wu, wd)
            return _ffn(x_mid, hn, wg, wu, wd)

        h, log_a = _inproj(xp, g1, w_main, w_lr, w_a2, b_a)
        og, s_fin = _gla_prompt(h, log_a, gain, batch, seq)
        om = _moba_prompt(h, batch, seq)
        xp = mixer(xp, _merge(og, om, h, w_a, w_b))
        k_p.append(h[:, H_KM:H_VM].reshape(batch, seq, MOBA_HEADS, MOBA_DH))
        v_p.append(h[:, H_VM:H_GA].reshape(batch, seq, MOBA_HEADS, MOBA_DH))
        s_p.append(s_fin)

        h, log_a = _inproj(xs, g1, w_main, w_lr, w_a2, b_a)
        og, s_new = _gla_sample(h, log_a, gain, state_gla, l, n_seq, t_new)
        om = _moba_sample(h, cache_k, cache_v, page_table, l, n_seq, t_new)
        xs = mixer(xs, _merge(og, om, h, w_a, w_b))
        k_s.append(h[:, H_KM:H_VM].reshape(n_seq, t_new, MOBA_HEADS, MOBA_DH))
        v_s.append(h[:, H_VM:H_GA].reshape(n_seq, t_new, MOBA_HEADS, MOBA_DH))
        s_s.append(s_new)

    g_fin = final_g[None, :]
    y_prompt = _final_norm(xp, g_fin).reshape(batch, seq, d)
    y_sample = _final_norm(xs, g_fin).reshape(n_seq, t_new, d)
    return (y_prompt, y_sample, jnp.stack(k_p), jnp.stack(v_p), jnp.stack(s_p),
            jnp.stack(k_s), jnp.stack(v_s), jnp.stack(s_s))
```

```python
import functools

import jax
import jax.numpy as jnp
from jax import lax
from jax.experimental import pallas as pl
from jax.experimental.pallas import tpu as pltpu

F32 = jnp.float32
MXU_DTYPE = jnp.bfloat16

GLA_HEADS = 4
GLA_DK = 256
GLA_DV = 512
GLA_RANK = 16
GLA_TAU = 16.0
MOBA_HEADS = 16
MOBA_DH = 128
MOBA_BLOCK = 256
MOBA_TOPK = 3
PAGE_SIZE = 128
N_EXPERTS = 8
RMS_EPS = 1e-6

LANES = 128
VMEM_LIMIT_BYTES = 56 * 1024 * 1024

H_QG, H_KG, H_VG, H_RG, H_QM, H_KM, H_VM, H_GA, H_GB = (
    0, 1024, 2048, 4096, 6144, 8192, 10240, 12288, 14336)
H_COLS = 16384

GLA_CHUNK = 128
GLA_SUB = 16
GLA_FAST_RANGE = 40.0
ROW_TILE = 512
NEG_INF = float("-inf")


def _params(*sem):
    return pltpu.CompilerParams(dimension_semantics=sem, vmem_limit_bytes=VMEM_LIMIT_BYTES)


def _mm(a, b):
    return jnp.dot(a.astype(MXU_DTYPE), b.astype(MXU_DTYPE), preferred_element_type=F32)


def _mm_nt(a, b):
    return lax.dot_general(a.astype(MXU_DTYPE), b.astype(MXU_DTYPE), (((1,), (1,)), ((), ())),
                           preferred_element_type=F32)


def _split3(a):
    hi = a.astype(MXU_DTYPE)
    r1 = a - hi.astype(F32)
    mid = r1.astype(MXU_DTYPE)
    lo = (r1 - mid.astype(F32)).astype(MXU_DTYPE)
    return hi, mid, lo


def _mm_f32(a, b, nt=False):
    f = _mm_nt if nt else _mm
    a0, a1, a2 = _split3(a)
    b0, b1, b2 = _split3(b)
    out = f(a0, b0)
    out = out + (f(a0, b1) + f(a1, b0))
    out = out + (f(a0, b2) + f(a2, b0) + f(a1, b1))
    return out


def _mm_exact_lhs(a01, b):
    b0, b1, b2 = _split3(b)
    a = a01.astype(MXU_DTYPE)
    return _mm(a, b0) + _mm(a, b1) + _mm(a, b2)


def _iota(shape, dim):
    return lax.broadcasted_iota(jnp.int32, shape, dim)


def _inproj_kernel(x_ref, g_ref, w_ref, wlr_ref, wa2_ref, ba_ref, h_ref, la_ref, xn_ref):
    @pl.when(pl.program_id(1) == 0)
    def _():
        x = x_ref[...]
        ms = jnp.mean(x * x, axis=-1, keepdims=True)
        xn_ref[...] = ((x * lax.rsqrt(ms + RMS_EPS)) * g_ref[...]).astype(xn_ref.dtype)
        a_lr = jnp.dot(xn_ref[...], wlr_ref[...], preferred_element_type=F32)
        z = _mm(a_lr, wa2_ref[...]) + ba_ref[...]
        log_sig = jnp.minimum(z, 0.0) - jnp.log1p(jnp.exp(-jnp.abs(z)))
        la_ref[...] = log_sig * (1.0 / GLA_TAU)

    h_ref[...] = jnp.dot(xn_ref[...], w_ref[...], preferred_element_type=F32)


def _inproj(x, g, w_main, w_lr, w_a2, b_a, *, tn=1024):
    m, d = x.shape
    tm = ROW_TILE
    n_qk = w_a2.shape[1]
    return pl.pallas_call(
        _inproj_kernel,
        grid=(m // tm, H_COLS // tn),
        in_specs=[
            pl.BlockSpec((tm, d), lambda i, j: (i, 0)),
            pl.BlockSpec((1, d), lambda i, j: (0, 0)),
            pl.BlockSpec((d, tn), lambda i, j: (0, j)),
            pl.BlockSpec((d, LANES), lambda i, j: (0, 0)),
            pl.BlockSpec((LANES, n_qk), lambda i, j: (0, 0)),
            pl.BlockSpec((1, n_qk), lambda i, j: (0, 0)),
        ],
        out_specs=[
            pl.BlockSpec((tm, tn), lambda i, j: (i, j)),
            pl.BlockSpec((tm, n_qk), lambda i, j: (i, 0)),
        ],
        out_shape=[jax.ShapeDtypeStruct((m, H_COLS), F32), jax.ShapeDtypeStruct((m, n_qk), F32)],
        scratch_shapes=[pltpu.VMEM((tm, d), MXU_DTYPE)],
        compiler_params=_params("parallel", "arbitrary"),
        name="inproj",
    )(x, g, w_main, w_lr, w_a2, b_a)


def _gla_chunk_fast(q, k, v, b, s):
    n = q.shape[0]
    b_t = b.T
    b_last = b_t[:, n - 1:n]
    qd = q * jnp.exp(b) * (GLA_DK ** -0.5)
    k_inv = k * jnp.exp(-b)
    kd_t = k.T * jnp.exp(b_last - b_t)
    attn = jnp.where(_iota((n, n), 0) >= _iota((n, n), 1), _mm_nt(qd, k_inv), 0.0)
    o = _mm(qd, s) + _mm(attn, v)
    s_new = jnp.exp(b_last) * s + _mm(kd_t, v)
    return o, s_new


def _gla_chunk_exact(q, k, v, b, s):
    n = q.shape[0]
    c = GLA_SUB
    pad_rows = LANES - c
    outs = []
    b_prev = jnp.zeros((1, b.shape[1]), F32)
    for r0 in range(0, n, c):
        qs, ks, vs, bs = (a[r0:r0 + c, :] for a in (q, k, v, b))
        qs = qs * (GLA_DK ** -0.5)
        b_end = bs[c - 1:c, :]
        o_sub = _mm(qs * jnp.exp(bs - b_prev), s)
        rows = []
        s_idx = _iota((c, 1), 0)
        for t in range(c):
            dec = jnp.exp((bs[t:t + 1, :] - bs) + jnp.where(s_idx <= t, 0.0, NEG_INF))
            w = jnp.sum(qs[t:t + 1, :] * ks * dec, axis=1, keepdims=True)
            rows.append(jnp.sum(w * vs, axis=0, keepdims=True))
        outs.append(o_sub + jnp.concatenate(rows, axis=0))
        kd = ks * jnp.exp(b_end - bs)
        zeros = lambda w_: jnp.zeros((pad_rows, w_), F32)
        kd_t = jnp.concatenate([kd, zeros(kd.shape[1])], axis=0).T
        carry_t = jnp.concatenate([b_end - b_prev, zeros(kd.shape[1])], axis=0).T
        v_pad = jnp.concatenate([vs, zeros(vs.shape[1])], axis=0)
        s = jnp.exp(carry_t[:, 0:1]) * s + _mm(kd_t, v_pad)
        b_prev = b_end
    return jnp.concatenate(outs, axis=0), s


def _gla_chunk(q, k, v, la, s, write):
    n = q.shape[0]
    lower = jnp.where(_iota((n, n), 0) >= _iota((n, n), 1), 1.0, 0.0)
    b = _mm_exact_lhs(lower, la)
    in_range = jnp.min(b[n - 1:n, :]) >= -GLA_FAST_RANGE

    @pl.when(in_range)
    def _():
        write(*_gla_chunk_fast(q, k, v, b, s))

    @pl.when(jnp.logical_not(in_range))
    def _():
        write(*_gla_chunk_exact(q, k, v, b, s))


def _gla_post(o, r, gain):
    ms = jnp.mean(o * o, axis=-1, keepdims=True)
    return (o * lax.rsqrt(ms + RMS_EPS) * gain) * (r * jax.nn.sigmoid(r))


def _gla_prompt_kernel(q_ref, k_ref, v_ref, r_ref, la_ref, gain_ref, og_ref, s_ref):
    @pl.when(pl.program_id(2) == 0)
    def _():
        s_ref[...] = jnp.zeros_like(s_ref)

    def write(o, s_new):
        s_ref[...] = s_new
        og_ref[...] = _gla_post(o, r_ref[...], gain_ref[...]).astype(og_ref.dtype)

    _gla_chunk(q_ref[...], k_ref[...], v_ref[...], la_ref[...], s_ref[...], write)


def _gla_prompt(h, log_a, gain, batch, seq):
    m = h.shape[0]
    nc = seq // GLA_CHUNK
    cb = GLA_CHUNK
    row = lambda b, hd, c: b * nc + c
    return pl.pallas_call(
        _gla_prompt_kernel,
        grid=(batch, GLA_HEADS, nc),
        in_specs=[
            pl.BlockSpec((cb, GLA_DK), lambda b, hd, c: (row(b, hd, c), H_QG // GLA_DK + hd)),
            pl.BlockSpec((cb, GLA_DK), lambda b, hd, c: (row(b, hd, c), H_KG // GLA_DK + hd)),
            pl.BlockSpec((cb, GLA_DV), lambda b, hd, c: (row(b, hd, c), H_VG // GLA_DV + hd)),
            pl.BlockSpec((cb, GLA_DV), lambda b, hd, c: (row(b, hd, c), H_RG // GLA_DV + hd)),
            pl.BlockSpec((cb, GLA_DK), lambda b, hd, c: (row(b, hd, c), hd)),
            pl.BlockSpec((1, GLA_DV), lambda b, hd, c: (0, 0)),
        ],
        out_specs=[
            pl.BlockSpec((cb, GLA_DV), lambda b, hd, c: (row(b, hd, c), hd)),
            pl.BlockSpec((None, None, GLA_DK, GLA_DV), lambda b, hd, c: (b, hd, 0, 0)),
        ],
        out_shape=[
            jax.ShapeDtypeStruct((m, GLA_HEADS * GLA_DV), MXU_DTYPE),
            jax.ShapeDtypeStruct((batch, GLA_HEADS, GLA_DK, GLA_DV), F32),
        ],
        compiler_params=_params("parallel", "parallel", "arbitrary"),
        name="gla_prompt",
    )(h, h, h, h, log_a, gain)


def _gla_sample_kernel(q_ref, k_ref, v_ref, r_ref, la_ref, gain_ref, s0_ref, og_ref, s_ref):
    t = q_ref.shape[0]

    def pad(a):
        return jnp.concatenate([a, jnp.zeros((GLA_CHUNK - t, a.shape[1]), a.dtype)], axis=0)

    def write(o, s_new):
        s_ref[...] = s_new
        og_ref[...] = _gla_post(o[:t], r_ref[...], gain_ref[...]).astype(og_ref.dtype)

    _gla_chunk(pad(q_ref[...]), pad(k_ref[...]), pad(v_ref[...]), pad(la_ref[...]), s0_ref[...], write)


def _gla_sample(h, log_a, gain, state, layer, n_seq, t):
    m = h.shape[0]
    return pl.pallas_call(
        _gla_sample_kernel,
        grid=(n_seq, GLA_HEADS),
        in_specs=[
            pl.BlockSpec((t, GLA_DK), lambda s, hd: (s, H_QG // GLA_DK + hd)),
            pl.BlockSpec((t, GLA_DK), lambda s, hd: (s, H_KG // GLA_DK + hd)),
            pl.BlockSpec((t, GLA_DV), lambda s, hd: (s, H_VG // GLA_DV + hd)),
            pl.BlockSpec((t, GLA_DV), lambda s, hd: (s, H_RG // GLA_DV + hd)),
            pl.BlockSpec((t, GLA_DK), lambda s, hd: (s, hd)),
            pl.BlockSpec((1, GLA_DV), lambda s, hd: (0, 0)),
            pl.BlockSpec((None, None, None, GLA_DK, GLA_DV), lambda s, hd: (layer, s, hd, 0, 0)),
        ],
        out_specs=[
            pl.BlockSpec((t, GLA_DV), lambda s, hd: (s, hd)),
            pl.BlockSpec((None, None, GLA_DK, GLA_DV), lambda s, hd: (s, hd, 0, 0)),
        ],
        out_shape=[
            jax.ShapeDtypeStruct((m, GLA_HEADS * GLA_DV), F32),
            jax.ShapeDtypeStruct((n_seq, GLA_HEADS, GLA_DK, GLA_DV), F32),
        ],
        compiler_params=_params("parallel", "parallel"),
        name="gla_sample",
    )(h, h, h, h, log_a, gain, state)


def _select_topk(g):
    nb = g.shape[0]
    n_idx = _iota(g.shape, 0)
    rank = jnp.zeros(g.shape, F32)
    for mth in range(nb):
        gm = g[mth:mth + 1, :]
        ahead = jnp.where(gm > g, 1.0, jnp.where(gm == g, jnp.where(n_idx > mth, 1.0, 0.0), 0.0))
        rank = rank + ahead
    return jnp.where(rank < float(MOBA_TOPK), jnp.where(g > NEG_INF, 1.0, 0.0), 0.0)


def _moba_prompt_kernel(q_ref, k_ref, v_ref, o_ref, kb_ref, vt_ref):
    blk = MOBA_BLOCK
    nb = q_ref.shape[0] // blk
    scale = MOBA_DH ** -0.5

    kb_ref[...] = k_ref[...].astype(kb_ref.dtype)
    kmean = jnp.concatenate(
        [jnp.mean(k_ref[j * blk:(j + 1) * blk, :], axis=0, keepdims=True) for j in range(nb)], axis=0)
    for j in range(nb):
        vt_ref[:, j * blk:(j + 1) * blk] = v_ref[j * blk:(j + 1) * blk, :].T.astype(vt_ref.dtype)
    causal_bias = jnp.where(_iota((blk, blk), 0) <= _iota((blk, blk), 1), 0.0, NEG_INF)

    for i in range(nb):
        q_t = q_ref[i * blk:(i + 1) * blk, :].T
        n_keys = (i + 1) * blk
        s = jnp.dot(kb_ref[0:n_keys, :], q_t.astype(MXU_DTYPE), preferred_element_type=F32) * scale
        if i > 0:
            gate = _mm_f32(kmean, q_t)
            gate = jnp.where(_iota(gate.shape, 0) < i, gate, NEG_INF)
            sel = _select_topk(gate)
        parts = []
        for j in range(i + 1):
            bias = causal_bias if j == i else jnp.where(sel[j:j + 1, :] > 0.0, 0.0, NEG_INF)
            parts.append(s[j * blk:(j + 1) * blk, :] + bias)
        m = parts[0].max(axis=0, keepdims=True)
        for part in parts[1:]:
            m = jnp.maximum(m, part.max(axis=0, keepdims=True))
        probs = [jnp.exp(part - m) for part in parts]
        l = probs[0].sum(axis=0, keepdims=True)
        for pr in probs[1:]:
            l = l + pr.sum(axis=0, keepdims=True)
        p_all = jnp.concatenate([pr.astype(MXU_DTYPE) for pr in probs], axis=0)
        acc = jnp.dot(vt_ref[:, 0:n_keys], p_all, preferred_element_type=F32)
        o_ref[i * blk:(i + 1) * blk, :] = (acc / l).T.astype(o_ref.dtype)


def _moba_prompt(h, batch, seq):
    m = h.shape[0]
    cq, ck, cv = H_QM // MOBA_DH, H_KM // MOBA_DH, H_VM // MOBA_DH
    return pl.pallas_call(
        _moba_prompt_kernel,
        grid=(batch, MOBA_HEADS),
        in_specs=[
            pl.BlockSpec((seq, MOBA_DH), lambda b, hd: (b, cq + hd)),
            pl.BlockSpec((seq, MOBA_DH), lambda b, hd: (b, ck + hd)),
            pl.BlockSpec((seq, MOBA_DH), lambda b, hd: (b, cv + hd)),
        ],
        out_specs=pl.BlockSpec((seq, MOBA_DH), lambda b, hd: (b, hd)),
        out_shape=jax.ShapeDtypeStruct((m, MOBA_HEADS * MOBA_DH), MXU_DTYPE),
        scratch_shapes=[
            pltpu.VMEM((seq, MOBA_DH), MXU_DTYPE),
            pltpu.VMEM((MOBA_DH, seq), MXU_DTYPE),
        ],
        compiler_params=_params("parallel", "parallel"),
        name="moba_prompt",
    )(h, h, h)


def _moba_sample_kernel(pt_ref, q_ref, kn_ref, vn_ref, ka_ref, kb_ref, va_ref, vb_ref, o_ref,
                        qp_ref, qf_ref, st_ref, p_ref, pown_ref, ksum0_ref, ksum1_ref, acc_ref, linv_ref):
    del pt_ref
    p = pl.program_id(1)
    nblk = ksum0_ref.shape[0]
    t = q_ref.shape[0]
    n_pairs = MOBA_HEADS // 2
    lanes = MOBA_HEADS * t
    d = q_ref.shape[1]
    rows = 2 * PAGE_SIZE
    scale = MOBA_DH ** -0.5

    def pair_rows(page_ref, hp):
        return page_ref[pl.ds(hp, rows, stride=n_pairs), :]

    def lane_half(shape):
        return (_iota(shape, 1) // t) % 2

    def lane_rows(hd):
        start = ((hd % n_pairs) * 2 + hd // n_pairs) * t
        return slice(start, start + t)

    @pl.when(p == 0)
    def _():
        q_rep = jnp.concatenate([q_ref[...]] * MOBA_HEADS, axis=0)
        r_full = _iota((lanes, d), 0)
        head_of_row = r_full // (2 * t) + n_pairs * ((r_full // t) % 2)
        qf_ref[...] = jnp.where(_iota((lanes, d), 1) // MOBA_DH == head_of_row, q_rep, 0.0)
        dh = d // 2
        r_half = _iota((lanes, dh), 0)
        q_sel = jnp.where((r_half // t) % 2 == 0, q_rep[:, :dh], q_rep[:, dh:])
        qp_ref[...] = jnp.where(_iota((lanes, dh), 1) // MOBA_DH == r_half // (2 * t), q_sel, 0.0)

    @pl.when(p < nblk)
    def _():
        for half, ref in enumerate((ka_ref, kb_ref)):
            kp = jnp.concatenate([pair_rows(ref, hp) for hp in range(n_pairs)], axis=1)
            st = _mm_nt(kp, qp_ref[...]) * scale
            st_ref[pl.ds(pl.multiple_of((2 * p + half) * rows, rows), rows), :] = st
            grp = kp[0:8, :]
            for r in range(8, rows, 8):
                grp = grp + kp[r:r + 8, :]
            even = _iota(grp.shape, 0) % 2 == 0
            s0 = jnp.sum(jnp.where(even, grp, 0.0), axis=0, keepdims=True)
            s1 = jnp.sum(jnp.where(even, 0.0, grp), axis=0, keepdims=True)
            if half == 0:
                ksum0_ref[pl.ds(p, 1), :] = s0
                ksum1_ref[pl.ds(p, 1), :] = s1
            else:
                ksum0_ref[pl.ds(p, 1), :] = ksum0_ref[pl.ds(p, 1), :] + s0
                ksum1_ref[pl.ds(p, 1), :] = ksum1_ref[pl.ds(p, 1), :] + s1

    @pl.when(p == nblk - 1)
    def _():
        inv_blk = 1.0 / MOBA_BLOCK
        g0 = _mm_f32(ksum0_ref[...] * inv_blk, qp_ref[...], nt=True)
        g1 = _mm_f32(ksum1_ref[...] * inv_blk, qp_ref[...], nt=True)
        gate = jnp.where(lane_half(g0.shape) == 0, g0, g1)
        sel = _select_topk(gate)
        s_own = _mm_nt(kn_ref[...], qf_ref[...]) * scale
        s_own = jnp.where(_iota((t, lanes), 0) <= _iota((t, lanes), 1) % t, s_own, NEG_INF)
        m = jnp.max(s_own, axis=0, keepdims=True)
        brows = 2 * rows
        half_bias = jnp.where(_iota((brows, lanes), 0) % 2 == lane_half((brows, lanes)), 0.0, NEG_INF)

        def masked(j):
            bias = jnp.where(sel[j:j + 1, :] > 0.0, 0.0, NEG_INF)
            return st_ref[j * brows:(j + 1) * brows, :] + (half_bias + bias)

        for j in range(nblk):
            m = jnp.maximum(m, jnp.max(masked(j), axis=0, keepdims=True))
        e_own = jnp.exp(s_own - m)
        l = jnp.sum(e_own, axis=0, keepdims=True)
        for j in range(nblk):
            ej = jnp.exp(masked(j) - m)
            p_ref[j * brows:(j + 1) * brows, :] = ej
            l = l + jnp.sum(ej, axis=0, keepdims=True)
        linv_ref[...] = jnp.broadcast_to(1.0 / l, (lanes, lanes)).T
        pown_ref[...] = jnp.concatenate([e_own, jnp.zeros((PAGE_SIZE - t, lanes), F32)], axis=0)
        acc_ref[...] = jnp.zeros_like(acc_ref)

    @pl.when(p >= nblk)
    def _():
        for half, ref in enumerate((va_ref, vb_ref)):
            off = pl.multiple_of((2 * (p - nblk) + half) * rows, rows)
            prob_t = p_ref[pl.ds(off, rows), :].T
            for hp in range(n_pairs):
                lo, hi = hp * 2 * t, (hp + 1) * 2 * t
                acc_ref[lo:hi, :] = acc_ref[lo:hi, :] + _mm(prob_t[lo:hi, :], pair_rows(ref, hp))

    @pl.when(p == 2 * nblk - 1)
    def _():
        v_own = jnp.concatenate([vn_ref[...], jnp.zeros((PAGE_SIZE - t, d), F32)], axis=0)
        pown_t = pown_ref[...].T
        for hd in range(MOBA_HEADS):
            out_h = acc_ref[lane_rows(hd), :] + _mm(pown_t[lane_rows(hd), :], v_own[:, hd * MOBA_DH:(hd + 1) * MOBA_DH])
            o_ref[:, hd * MOBA_DH:(hd + 1) * MOBA_DH] = (out_h * linv_ref[lane_rows(hd), :]).astype(o_ref.dtype)


def _moba_sample(h, cache_k, cache_v, page_table, layer, n_seq, t):
    m = h.shape[0]
    d = MOBA_HEADS * MOBA_DH
    n_pages = page_table.shape[1]
    nblk = n_pages * PAGE_SIZE // MOBA_BLOCK
    ppb = MOBA_BLOCK // PAGE_SIZE
    assert ppb == 2
    hq = MOBA_HEADS * t
    assert hq == LANES
    ck = cache_k.reshape(cache_k.shape[0], cache_k.shape[1], PAGE_SIZE * MOBA_HEADS, MOBA_DH)
    cv = cache_v.reshape(cache_v.shape[0], cache_v.shape[1], PAGE_SIZE * MOBA_HEADS, MOBA_DH)

    def kpage(half):
        return lambda s, p, pt: (layer, pt[s, ppb * jnp.minimum(p, nblk - 1) + half], 0, 0)

    def vpage(half):
        return lambda s, p, pt: (layer, pt[s, ppb * jnp.maximum(p - nblk, 0) + half], 0, 0)

    page_spec = lambda f: pl.BlockSpec((None, None, PAGE_SIZE * MOBA_HEADS, MOBA_DH), f)
    grid_spec = pltpu.PrefetchScalarGridSpec(
        num_scalar_prefetch=1,
        grid=(n_seq, 2 * nblk),
        in_specs=[
            pl.BlockSpec((t, d), lambda s, p, pt: (s, H_QM // d)),
            pl.BlockSpec((t, d), lambda s, p, pt: (s, H_KM // d)),
            pl.BlockSpec((t, d), lambda s, p, pt: (s, H_VM // d)),
            page_spec(kpage(0)), page_spec(kpage(1)), page_spec(vpage(0)), page_spec(vpage(1)),
        ],
        out_specs=pl.BlockSpec((t, d), lambda s, p, pt: (s, 0)),
        scratch_shapes=[
            pltpu.VMEM((hq, d // 2), F32),
            pltpu.VMEM((hq, d), F32),
            pltpu.VMEM((n_pages * 2 * PAGE_SIZE, hq), F32),
            pltpu.VMEM((n_pages * 2 * PAGE_SIZE, hq), F32),
            pltpu.VMEM((PAGE_SIZE, hq), F32),
            pltpu.VMEM((nblk, d // 2), F32),
            pltpu.VMEM((nblk, d // 2), F32),
            pltpu.VMEM((hq, MOBA_DH), F32),
            pltpu.VMEM((hq, hq), F32),
        ],
    )
    return pl.pallas_call(
        _moba_sample_kernel,
        grid_spec=grid_spec,
        out_shape=jax.ShapeDtypeStruct((m, d), F32),
        compiler_params=_params("parallel", "arbitrary"),
        name="moba_sample",
    )(page_table, h, h, h, ck, ck, cv, cv)


def _merge_kernel(og_ref, om_ref, ga_ref, gb_ref, wa_ref, wb_ref, m_ref):
    ya = _mm(og_ref[...], wa_ref[...])
    yb = _mm(om_ref[...], wb_ref[...])
    m_ref[...] = (jax.nn.sigmoid(ga_ref[...]) * ya + jax.nn.sigmoid(gb_ref[...]) * yb).astype(m_ref.dtype)


def _merge(og, om, h, w_a, w_b, *, tn=512):
    m, d = og.shape
    tm = ROW_TILE
    return pl.pallas_call(
        _merge_kernel,
        grid=(m // tm, d // tn),
        in_specs=[
            pl.BlockSpec((tm, d), lambda i, j: (i, 0)),
            pl.BlockSpec((tm, d), lambda i, j: (i, 0)),
            pl.BlockSpec((tm, tn), lambda i, j: (i, H_GA // tn + j)),
            pl.BlockSpec((tm, tn), lambda i, j: (i, H_GB // tn + j)),
            pl.BlockSpec((d, tn), lambda i, j: (0, j)),
            pl.BlockSpec((d, tn), lambda i, j: (0, j)),
        ],
        out_specs=pl.BlockSpec((tm, tn), lambda i, j: (i, j)),
        out_shape=jax.ShapeDtypeStruct((m, d), MXU_DTYPE),
        compiler_params=_params("parallel", "arbitrary"),
        name="merge",
    )(og, om, h, h, w_a, w_b)


def _outproj_kernel(x_ref, m_ref, wo_ref, g_ref, *rest):
    x = x_ref[...] + jnp.dot(m_ref[...], wo_ref[...], preferred_element_type=F32)
    ms = jnp.mean(x * x, axis=-1, keepdims=True)
    hn = (x * lax.rsqrt(ms + RMS_EPS)) * g_ref[...]
    if len(rest) == 2:
        xo_ref, hn_ref = rest
    else:
        router_ref, xo_ref, hn_ref, logit_ref = rest
        logit_ref[...] = _mm_f32(hn, router_ref[...])
    xo_ref[...] = x
    hn_ref[...] = hn.astype(hn_ref.dtype)


def _outproj(x, mix, w_o, g2, router_pad=None):
    m, d = x.shape
    tm = ROW_TILE
    row_spec = lambda w: pl.BlockSpec((tm, w), lambda i: (i, 0))
    in_specs = [row_spec(d), row_spec(d), pl.BlockSpec((d, d), lambda i: (0, 0)), pl.BlockSpec((1, d), lambda i: (0, 0))]
    out_specs = [row_spec(d), row_spec(d)]
    out_shape = [jax.ShapeDtypeStruct((m, d), F32), jax.ShapeDtypeStruct((m, d), MXU_DTYPE)]
    args = [x, mix, w_o, g2]
    if router_pad is not None:
        in_specs.append(pl.BlockSpec((d, LANES), lambda i: (0, 0)))
        out_specs.append(row_spec(LANES))
        out_shape.append(jax.ShapeDtypeStruct((m, LANES), F32))
        args.append(router_pad)
    return pl.pallas_call(
        _outproj_kernel,
        grid=(m // tm,),
        in_specs=in_specs,
        out_specs=out_specs,
        out_shape=out_shape,
        compiler_params=_params("parallel"),
        name="outproj",
    )(*args)


def _ffn_kernel(x_ref, hn_ref, wg_ref, wu_ref, wd_ref, o_ref):
    @pl.when(pl.program_id(1) == 0)
    def _():
        o_ref[...] = x_ref[...]

    hn = hn_ref[...]
    gate = jnp.dot(hn, wg_ref[...], preferred_element_type=F32)
    up = jnp.dot(hn, wu_ref[...], preferred_element_type=F32)
    act = (gate * jax.nn.sigmoid(gate)) * up
    o_ref[...] += _mm(act, wd_ref[...])


def _ffn(x, hn, wg, wu, wd, *, tf=512):
    m, d = x.shape
    f = wg.shape[1]
    tm = ROW_TILE
    return pl.pallas_call(
        _ffn_kernel,
        grid=(m // tm, f // tf),
        in_specs=[
            pl.BlockSpec((tm, d), lambda i, j: (i, 0)),
            pl.BlockSpec((tm, d), lambda i, j: (i, 0)),
            pl.BlockSpec((d, tf), lambda i, j: (0, j)),
            pl.BlockSpec((d, tf), lambda i, j: (0, j)),
            pl.BlockSpec((tf, d), lambda i, j: (j, 0)),
        ],
        out_specs=pl.BlockSpec((tm, d), lambda i, j: (i, 0)),
        out_shape=jax.ShapeDtypeStruct((m, d), F32),
        compiler_params=_params("parallel", "arbitrary"),
        name="ffn",
    )(x, hn, wg, wu, wd)


def _moe_kernel(x_ref, hn_ref, logit_ref, wg_ref, wu_ref, wd_ref, o_ref, gates_ref, acc_ref):
    e, j = pl.program_id(1), pl.program_id(2)
    nj = pl.num_programs(2)

    @pl.when((e == 0) & (j == 0))
    def _():
        o_ref[...] = x_ref[...]
        lg = logit_ref[...]
        lane = _iota(lg.shape, 1)
        lg = jnp.where(lane < N_EXPERTS, lg, NEG_INF)
        m1 = jnp.max(lg, axis=-1, keepdims=True)
        i1 = jnp.min(jnp.where(lg == m1, lane, LANES), axis=-1, keepdims=True)
        lg2 = jnp.where(lane == i1, NEG_INF, lg)
        m2 = jnp.max(lg2, axis=-1, keepdims=True)
        i2 = jnp.min(jnp.where(lg2 == m2, lane, LANES), axis=-1, keepdims=True)
        e2 = jnp.exp(m2 - m1)
        w1 = 1.0 / (1.0 + e2)
        w2 = e2 / (1.0 + e2)
        gates_ref[...] = jnp.where(lane == i1, w1, 0.0) + jnp.where(lane == i2, w2, 0.0)

    hn = hn_ref[...]
    gate = jnp.dot(hn, wg_ref[...], preferred_element_type=F32)
    up = jnp.dot(hn, wu_ref[...], preferred_element_type=F32)
    act = (gate * jax.nn.sigmoid(gate)) * up
    y = _mm(act, wd_ref[...])

    @pl.when(j == 0)
    def _():
        acc_ref[...] = y

    @pl.when(j > 0)
    def _():
        acc_ref[...] += y

    @pl.when(j == nj - 1)
    def _():
        gates = gates_ref[...]
        ge = jnp.sum(jnp.where(_iota(gates.shape, 1) == e, gates, 0.0), axis=-1, keepdims=True)
        o_ref[...] += ge * acc_ref[...]


def _moe(x, hn, logits, wg, wu, wd, *, tf=768):
    m, d = x.shape
    ne, _, f = wg.shape
    tm = ROW_TILE
    return pl.pallas_call(
        _moe_kernel,
        grid=(m // tm, ne, f // tf),
        in_specs=[
            pl.BlockSpec((tm, d), lambda i, e, j: (i, 0)),
            pl.BlockSpec((tm, d), lambda i, e, j: (i, 0)),
            pl.BlockSpec((tm, LANES), lambda i, e, j: (i, 0)),
            pl.BlockSpec((None, d, tf), lambda i, e, j: (e, 0, j)),
            pl.BlockSpec((None, d, tf), lambda i, e, j: (e, 0, j)),
            pl.BlockSpec((None, tf, d), lambda i, e, j: (e, j, 0)),
        ],
        out_specs=pl.BlockSpec((tm, d), lambda i, e, j: (i, 0)),
        out_shape=jax.ShapeDtypeStruct((m, d), F32),
        scratch_shapes=[pltpu.VMEM((tm, LANES), F32), pltpu.VMEM((tm, d), F32)],
        compiler_params=_params("parallel", "arbitrary", "arbitrary"),
        name="moe",
    )(x, hn, logits, wg, wu, wd)


def _norm_kernel(x_ref, g_ref, o_ref):
    x = x_ref[...]
    ms = jnp.mean(x * x, axis=-1, keepdims=True)
    o_ref[...] = (x * lax.rsqrt(ms + RMS_EPS)) * g_ref[...]


def _final_norm(x, g):
    m, d = x.shape
    tm = ROW_TILE
    return pl.pallas_call(
        _norm_kernel,
        grid=(m // tm,),
        in_specs=[pl.BlockSpec((tm, d), lambda i: (i, 0)), pl.BlockSpec((1, d), lambda i: (0, 0))],
        out_specs=pl.BlockSpec((tm, d), lambda i: (i, 0)),
        out_shape=jax.ShapeDtypeStruct((m, d), F32),
        compiler_params=_params("parallel"),
        name="final_norm",
    )(x, g)


def _pad_to(a, axis, size):
    pad = [(0, 0)] * a.ndim
    pad[axis] = (0, size - a.shape[axis])
    return jnp.pad(a, pad)


def kernel(x_prompt, x_sample, cache_k, cache_v, state_gla, page_table, norm1_g, w_in, gla_w_a2, gla_b_a,
           gla_norm_g, w_gla_br, w_moba_br, w_out, norm2_g, w_gate_d, w_up_d, w_down_d, router,
           w_gate_e, w_up_e, w_down_e, final_g):
    batch, seq, d = x_prompt.shape
    n_seq, t_new, _ = x_sample.shape
    depth = w_in.shape[0]
    lr_lo = 2 * GLA_HEADS * GLA_DK + GLA_HEADS * GLA_DV
    lr_hi = lr_lo + GLA_RANK
    f_pad = -(-w_gate_e.shape[-1] // 768) * 768

    xp = x_prompt.reshape(batch * seq, d)
    xs = x_sample.reshape(n_seq * t_new, d)
    k_p, v_p, s_p, k_s, v_s, s_s = [], [], [], [], [], []
    for l in range(depth):
        bf = lambda a: a.astype(MXU_DTYPE)
        w_main = bf(jnp.concatenate([w_in[l, :, :lr_lo], w_in[l, :, lr_hi:]], axis=1))
        w_lr = bf(_pad_to(w_in[l, :, lr_lo:lr_hi], 1, LANES))
        w_a2 = bf(_pad_to(gla_w_a2[l], 0, LANES))
        b_a = gla_b_a[l][None, :]
        g1 = norm1_g[l][None, :]
        g2 = norm2_g[l][None, :]
        gain = gla_norm_g[l][None, :]
        w_a, w_b, w_o = bf(w_gla_br[l]), bf(w_moba_br[l]), bf(w_out[l])
        is_moe = l % 2 == 1
        li = l // 2
        if is_moe:
            router_pad = _pad_to(router[li], 1, LANES)
            wg = bf(_pad_to(w_gate_e[li], 2, f_pad))
            wu = bf(_pad_to(w_up_e[li], 2, f_pad))
            wd = bf(_pad_to(w_down_e[li], 1, f_pad))
        else:
            router_pad = None
            wg, wu, wd = bf(w_gate_d[li]), bf(w_up_d[li]), bf(w_down_d[li])

        def mixer(x, mix):
            if is_moe:
                x_mid, hn, logits = _outproj(x, mix, w_o, g2, router_pad)
                return _moe(x_mid, hn, logits, wg, wu, wd)
            x_mid, hn = _outproj(x, mix, w_o, g2)
            return _ffn(x_mid, hn, wg, wu, wd)

        h, log_a = _inproj(xp, g1, w_main, w_lr, w_a2, b_a)
        og, s_fin = _gla_prompt(h, log_a, gain, batch, seq)
        om = _moba_prompt(h, batch, seq)
        xp = mixer(xp, _merge(og, om, h, w_a, w_b))
        k_p.append(h[:, H_KM:H_VM].reshape(batch, seq, MOBA_HEADS, MOBA_DH))
        v_p.append(h[:, H_VM:H_GA].reshape(batch, seq, MOBA_HEADS, MOBA_DH))
        s_p.append(s_fin)

        h, log_a = _inproj(xs, g1, w_main, w_lr, w_a2, b_a)
        og, s_new = _gla_sample(h, log_a, gain, state_gla, l, n_seq, t_new)
        om = _moba_sample(h, cache_k, cache_v, page_table, l, n_seq, t_new)
        xs = mixer(xs, _merge(og, om, h, w_a, w_b))
        k_s.append(h[:, H_KM:H_VM].reshape(n_seq, t_new, MOBA_HEADS, MOBA_DH))
        v_s.append(h[:, H_VM:H_GA].reshape(n_seq, t_new, MOBA_HEADS, MOBA_DH))
        s_s.append(s_new)

    g_fin = final_g[None, :]
    y_prompt = _final_norm(xp, g_fin).reshape(batch, seq, d)
    y_sample = _final_norm(xs, g_fin).reshape(n_seq, t_new, d)
    return (y_prompt, y_sample, jnp.stack(k_p), jnp.stack(v_p), jnp.stack(s_p),
            jnp.stack(k_s), jnp.stack(v_s), jnp.stack(s_s))
```

```python
import functools

import jax
import jax.numpy as jnp
from jax import lax
from jax.experimental import pallas as pl
from jax.experimental.pallas import tpu as pltpu

F32 = jnp.float32
MXU_DTYPE = jnp.bfloat16

GLA_HEADS = 4
GLA_DK = 256
GLA_DV = 512
GLA_RANK = 16
GLA_TAU = 16.0
MOBA_HEADS = 16
MOBA_DH = 128
MOBA_BLOCK = 256
MOBA_TOPK = 3
PAGE_SIZE = 128
N_EXPERTS = 8
RMS_EPS = 1e-6

LANES = 128
VMEM_LIMIT_BYTES = 56 * 1024 * 1024

H_QG, H_KG, H_VG, H_RG, H_QM, H_KM, H_VM, H_GA, H_GB = (
    0, 1024, 2048, 4096, 6144, 8192, 10240, 12288, 14336)
H_COLS = 16384

GLA_CHUNK = 128
GLA_SUB = 16
GLA_FAST_RANGE = 40.0
ROW_TILE = 512
INPROJ_ROW_TILE = 1024
SAMPLE_PAGES_PER_STEP = 4
NEG_INF = float("-inf")


def _params(*sem):
    return pltpu.CompilerParams(dimension_semantics=sem, vmem_limit_bytes=VMEM_LIMIT_BYTES)


def _mm(a, b):
    return jnp.dot(a.astype(MXU_DTYPE), b.astype(MXU_DTYPE), preferred_element_type=F32)


def _mm_nt(a, b):
    return lax.dot_general(a.astype(MXU_DTYPE), b.astype(MXU_DTYPE), (((1,), (1,)), ((), ())),
                           preferred_element_type=F32)


def _split3(a):
    hi = a.astype(MXU_DTYPE)
    r1 = a - hi.astype(F32)
    mid = r1.astype(MXU_DTYPE)
    lo = (r1 - mid.astype(F32)).astype(MXU_DTYPE)
    return hi, mid, lo


def _mm_f32(a, b, nt=False):
    f = _mm_nt if nt else _mm
    a0, a1, a2 = _split3(a)
    b0, b1, b2 = _split3(b)
    out = f(a0, b0)
    out = out + (f(a0, b1) + f(a1, b0))
    out = out + (f(a0, b2) + f(a2, b0) + f(a1, b1))
    return out


def _mm_exact_lhs(a01, b):
    b0, b1, b2 = _split3(b)
    a = a01.astype(MXU_DTYPE)
    return _mm(a, b0) + _mm(a, b1) + _mm(a, b2)


def _iota(shape, dim):
    return lax.broadcasted_iota(jnp.int32, shape, dim)


def _inproj_kernel(x_ref, g_ref, w_ref, wlr_ref, wa2_ref, ba_ref, h_ref, la_ref, xn_ref):
    @pl.when(pl.program_id(1) == 0)
    def _():
        x = x_ref[...]
        ms = jnp.mean(x * x, axis=-1, keepdims=True)
        xn_ref[...] = ((x * lax.rsqrt(ms + RMS_EPS)) * g_ref[...]).astype(xn_ref.dtype)
        a_lr = jnp.dot(xn_ref[...], wlr_ref[...], preferred_element_type=F32)
        z = _mm(a_lr, wa2_ref[...]) + ba_ref[...]
        log_sig = jnp.minimum(z, 0.0) - jnp.log1p(jnp.exp(-jnp.abs(z)))
        la_ref[...] = log_sig * (1.0 / GLA_TAU)

    h_ref[...] = jnp.dot(xn_ref[...], w_ref[...], preferred_element_type=F32)


def _inproj(x, g, w_main, w_lr, w_a2, b_a, *, tn=1024):
    m, d = x.shape
    tm = INPROJ_ROW_TILE
    n_qk = w_a2.shape[1]
    return pl.pallas_call(
        _inproj_kernel,
        grid=(m // tm, H_COLS // tn),
        in_specs=[
            pl.BlockSpec((tm, d), lambda i, j: (i, 0)),
            pl.BlockSpec((1, d), lambda i, j: (0, 0)),
            pl.BlockSpec((d, tn), lambda i, j: (0, j)),
            pl.BlockSpec((d, LANES), lambda i, j: (0, 0)),
            pl.BlockSpec((LANES, n_qk), lambda i, j: (0, 0)),
            pl.BlockSpec((1, n_qk), lambda i, j: (0, 0)),
        ],
        out_specs=[
            pl.BlockSpec((tm, tn), lambda i, j: (i, j)),
            pl.BlockSpec((tm, n_qk), lambda i, j: (i, 0)),
        ],
        out_shape=[jax.ShapeDtypeStruct((m, H_COLS), F32), jax.ShapeDtypeStruct((m, n_qk), F32)],
        scratch_shapes=[pltpu.VMEM((tm, d), MXU_DTYPE)],
        compiler_params=_params("parallel", "arbitrary"),
        name="inproj",
    )(x, g, w_main, w_lr, w_a2, b_a)


def _gla_chunk_fast(q, k, v, b, s):
    n = q.shape[0]
    b_t = b.T
    b_last = b_t[:, n - 1:n]
    qd = q * jnp.exp(b) * (GLA_DK ** -0.5)
    k_inv = k * jnp.exp(-b)
    kd_t = k.T * jnp.exp(b_last - b_t)
    attn = jnp.where(_iota((n, n), 0) >= _iota((n, n), 1), _mm_nt(qd, k_inv), 0.0)
    o = _mm(qd, s) + _mm(attn, v)
    s_new = jnp.exp(b_last) * s + _mm(kd_t, v)
    return o, s_new


def _gla_chunk_exact(q, k, v, b, s):
    n = q.shape[0]
    c = GLA_SUB
    pad_rows = LANES - c
    outs = []
    b_prev = jnp.zeros((1, b.shape[1]), F32)
    for r0 in range(0, n, c):
        qs, ks, vs, bs = (a[r0:r0 + c, :] for a in (q, k, v, b))
        qs = qs * (GLA_DK ** -0.5)
        b_end = bs[c - 1:c, :]
        o_sub = _mm(qs * jnp.exp(bs - b_prev), s)
        rows = []
        s_idx = _iota((c, 1), 0)
        for t in range(c):
            dec = jnp.exp((bs[t:t + 1, :] - bs) + jnp.where(s_idx <= t, 0.0, NEG_INF))
            w = jnp.sum(qs[t:t + 1, :] * ks * dec, axis=1, keepdims=True)
            rows.append(jnp.sum(w * vs, axis=0, keepdims=True))
        outs.append(o_sub + jnp.concatenate(rows, axis=0))
        kd = ks * jnp.exp(b_end - bs)
        zeros = lambda w_: jnp.zeros((pad_rows, w_), F32)
        kd_t = jnp.concatenate([kd, zeros(kd.shape[1])], axis=0).T
        carry_t = jnp.concatenate([b_end - b_prev, zeros(kd.shape[1])], axis=0).T
        v_pad = jnp.concatenate([vs, zeros(vs.shape[1])], axis=0)
        s = jnp.exp(carry_t[:, 0:1]) * s + _mm(kd_t, v_pad)
        b_prev = b_end
    return jnp.concatenate(outs, axis=0), s


def _gla_chunk(q, k, v, la, s, write):
    n = q.shape[0]
    lower = jnp.where(_iota((n, n), 0) >= _iota((n, n), 1), 1.0, 0.0)
    b = _mm_exact_lhs(lower, la)
    in_range = jnp.min(b[n - 1:n, :]) >= -GLA_FAST_RANGE

    @pl.when(in_range)
    def _():
        write(*_gla_chunk_fast(q, k, v, b, s))

    @pl.when(jnp.logical_not(in_range))
    def _():
        write(*_gla_chunk_exact(q, k, v, b, s))


def _gla_post(o, r, gain):
    ms = jnp.mean(o * o, axis=-1, keepdims=True)
    return (o * lax.rsqrt(ms + RMS_EPS) * gain) * (r * jax.nn.sigmoid(r))


def _gla_prompt_kernel(q_ref, k_ref, v_ref, r_ref, la_ref, gain_ref, og_ref, s_ref):
    @pl.when(pl.program_id(2) == 0)
    def _():
        s_ref[...] = jnp.zeros_like(s_ref)

    def write(o, s_new):
        s_ref[...] = s_new
        og_ref[...] = _gla_post(o, r_ref[...], gain_ref[...]).astype(og_ref.dtype)

    _gla_chunk(q_ref[...], k_ref[...], v_ref[...], la_ref[...], s_ref[...], write)


def _gla_prompt(h, log_a, gain, batch, seq):
    m = h.shape[0]
    nc = seq // GLA_CHUNK
    cb = GLA_CHUNK
    row = lambda b, hd, c: b * nc + c
    return pl.pallas_call(
        _gla_prompt_kernel,
        grid=(batch, GLA_HEADS, nc),
        in_specs=[
            pl.BlockSpec((cb, GLA_DK), lambda b, hd, c: (row(b, hd, c), H_QG // GLA_DK + hd)),
            pl.BlockSpec((cb, GLA_DK), lambda b, hd, c: (row(b, hd, c), H_KG // GLA_DK + hd)),
            pl.BlockSpec((cb, GLA_DV), lambda b, hd, c: (row(b, hd, c), H_VG // GLA_DV + hd)),
            pl.BlockSpec((cb, GLA_DV), lambda b, hd, c: (row(b, hd, c), H_RG // GLA_DV + hd)),
            pl.BlockSpec((cb, GLA_DK), lambda b, hd, c: (row(b, hd, c), hd)),
            pl.BlockSpec((1, GLA_DV), lambda b, hd, c: (0, 0)),
        ],
        out_specs=[
            pl.BlockSpec((cb, GLA_DV), lambda b, hd, c: (row(b, hd, c), hd)),
            pl.BlockSpec((None, None, GLA_DK, GLA_DV), lambda b, hd, c: (b, hd, 0, 0)),
        ],
        out_shape=[
            jax.ShapeDtypeStruct((m, GLA_HEADS * GLA_DV), MXU_DTYPE),
            jax.ShapeDtypeStruct((batch, GLA_HEADS, GLA_DK, GLA_DV), F32),
        ],
        compiler_params=_params("parallel", "parallel", "arbitrary"),
        name="gla_prompt",
    )(h, h, h, h, log_a, gain)


def _gla_sample_kernel(q_ref, k_ref, v_ref, r_ref, la_ref, gain_ref, s0_ref, og_ref, s_ref):
    t = q_ref.shape[0]

    def pad(a):
        return jnp.concatenate([a, jnp.zeros((GLA_CHUNK - t, a.shape[1]), a.dtype)], axis=0)

    def write(o, s_new):
        s_ref[...] = s_new
        og_ref[...] = _gla_post(o[:t], r_ref[...], gain_ref[...]).astype(og_ref.dtype)

    _gla_chunk(pad(q_ref[...]), pad(k_ref[...]), pad(v_ref[...]), pad(la_ref[...]), s0_ref[...], write)


def _gla_sample(h, log_a, gain, state, layer, n_seq, t):
    m = h.shape[0]
    return pl.pallas_call(
        _gla_sample_kernel,
        grid=(n_seq, GLA_HEADS),
        in_specs=[
            pl.BlockSpec((t, GLA_DK), lambda s, hd: (s, H_QG // GLA_DK + hd)),
            pl.BlockSpec((t, GLA_DK), lambda s, hd: (s, H_KG // GLA_DK + hd)),
            pl.BlockSpec((t, GLA_DV), lambda s, hd: (s, H_VG // GLA_DV + hd)),
            pl.BlockSpec((t, GLA_DV), lambda s, hd: (s, H_RG // GLA_DV + hd)),
            pl.BlockSpec((t, GLA_DK), lambda s, hd: (s, hd)),
            pl.BlockSpec((1, GLA_DV), lambda s, hd: (0, 0)),
            pl.BlockSpec((None, None, None, GLA_DK, GLA_DV), lambda s, hd: (layer, s, hd, 0, 0)),
        ],
        out_specs=[
            pl.BlockSpec((t, GLA_DV), lambda s, hd: (s, hd)),
            pl.BlockSpec((None, None, GLA_DK, GLA_DV), lambda s, hd: (s, hd, 0, 0)),
        ],
        out_shape=[
            jax.ShapeDtypeStruct((m, GLA_HEADS * GLA_DV), F32),
            jax.ShapeDtypeStruct((n_seq, GLA_HEADS, GLA_DK, GLA_DV), F32),
        ],
        compiler_params=_params("parallel", "parallel"),
        name="gla_sample",
    )(h, h, h, h, log_a, gain, state)


def _select_topk(g):
    nb = g.shape[0]
    n_idx = _iota(g.shape, 0)
    rank = jnp.zeros(g.shape, F32)
    for mth in range(nb):
        gm = g[mth:mth + 1, :]
        ahead = jnp.where(gm > g, 1.0, jnp.where(gm == g, jnp.where(n_idx > mth, 1.0, 0.0), 0.0))
        rank = rank + ahead
    return jnp.where(rank < float(MOBA_TOPK), jnp.where(g > NEG_INF, 1.0, 0.0), 0.0)


def _moba_prompt_kernel(q_ref, k_ref, v_ref, o_ref, kb_ref, vt_ref):
    blk = MOBA_BLOCK
    nb = q_ref.shape[0] // blk
    scale = MOBA_DH ** -0.5

    kb_ref[...] = k_ref[...].astype(kb_ref.dtype)
    kmean = jnp.concatenate(
        [jnp.mean(k_ref[j * blk:(j + 1) * blk, :], axis=0, keepdims=True) for j in range(nb)], axis=0)
    for j in range(nb):
        vt_ref[:, j * blk:(j + 1) * blk] = v_ref[j * blk:(j + 1) * blk, :].T.astype(vt_ref.dtype)
    causal_bias = jnp.where(_iota((blk, blk), 0) <= _iota((blk, blk), 1), 0.0, NEG_INF)

    for i in range(nb):
        q_t = q_ref[i * blk:(i + 1) * blk, :].T
        n_keys = (i + 1) * blk
        s = jnp.dot(kb_ref[0:n_keys, :], q_t.astype(MXU_DTYPE), preferred_element_type=F32) * scale
        if i > 0:
            gate = _mm_f32(kmean, q_t)
            gate = jnp.where(_iota(gate.shape, 0) < i, gate, NEG_INF)
            sel = _select_topk(gate)
        parts = []
        for j in range(i + 1):
            bias = causal_bias if j == i else jnp.where(sel[j:j + 1, :] > 0.0, 0.0, NEG_INF)
            parts.append(s[j * blk:(j + 1) * blk, :] + bias)
        m = parts[0].max(axis=0, keepdims=True)
        for part in parts[1:]:
            m = jnp.maximum(m, part.max(axis=0, keepdims=True))
        probs = [jnp.exp(part - m) for part in parts]
        l = probs[0].sum(axis=0, keepdims=True)
        for pr in probs[1:]:
            l = l + pr.sum(axis=0, keepdims=True)
        p_all = jnp.concatenate([pr.astype(MXU_DTYPE) for pr in probs], axis=0)
        acc = jnp.dot(vt_ref[:, 0:n_keys], p_all, preferred_element_type=F32)
        o_ref[i * blk:(i + 1) * blk, :] = (acc / l).T.astype(o_ref.dtype)


def _moba_prompt(h, batch, seq):
    m = h.shape[0]
    cq, ck, cv = H_QM // MOBA_DH, H_KM // MOBA_DH, H_VM // MOBA_DH
    return pl.pallas_call(
        _moba_prompt_kernel,
        grid=(batch, MOBA_HEADS),
        in_specs=[
            pl.BlockSpec((seq, MOBA_DH), lambda b, hd: (b, cq + hd)),
            pl.BlockSpec((seq, MOBA_DH), lambda b, hd: (b, ck + hd)),
            pl.BlockSpec((seq, MOBA_DH), lambda b, hd: (b, cv + hd)),
        ],
        out_specs=pl.BlockSpec((seq, MOBA_DH), lambda b, hd: (b, hd)),
        out_shape=jax.ShapeDtypeStruct((m, MOBA_HEADS * MOBA_DH), MXU_DTYPE),
        scratch_shapes=[
            pltpu.VMEM((seq, MOBA_DH), MXU_DTYPE),
            pltpu.VMEM((MOBA_DH, seq), MXU_DTYPE),
        ],
        compiler_params=_params("parallel", "parallel"),
        name="moba_prompt",
    )(h, h, h)


def _moba_sample_kernel(pt_ref, q_ref, kn_ref, vn_ref, *rest):
    del pt_ref
    pps = SAMPLE_PAGES_PER_STEP
    k_refs, v_refs = rest[:pps], rest[pps:2 * pps]
    o_ref, qp_ref, qf_ref, st_ref, p_ref, pown_ref, ksum0_ref, ksum1_ref, acc_ref, linv_ref = rest[2 * pps:]
    p = pl.program_id(1)
    nblk = ksum0_ref.shape[0]
    ppb = MOBA_BLOCK // PAGE_SIZE
    ksteps = nblk * ppb // pps
    t = q_ref.shape[0]
    n_pairs = MOBA_HEADS // 2
    lanes = MOBA_HEADS * t
    d = q_ref.shape[1]
    rows = 2 * PAGE_SIZE
    scale = MOBA_DH ** -0.5

    def pair_rows(page_ref, hp):
        return page_ref[pl.ds(hp, rows, stride=n_pairs), :]

    def lane_half(shape):
        return (_iota(shape, 1) // t) % 2

    def lane_rows(hd):
        start = ((hd % n_pairs) * 2 + hd // n_pairs) * t
        return slice(start, start + t)

    @pl.when(p == 0)
    def _():
        q_rep = jnp.concatenate([q_ref[...]] * MOBA_HEADS, axis=0)
        r_full = _iota((lanes, d), 0)
        head_of_row = r_full // (2 * t) + n_pairs * ((r_full // t) % 2)
        qf_ref[...] = jnp.where(_iota((lanes, d), 1) // MOBA_DH == head_of_row, q_rep, 0.0)
        dh = d // 2
        r_half = _iota((lanes, dh), 0)
        q_sel = jnp.where((r_half // t) % 2 == 0, q_rep[:, :dh], q_rep[:, dh:])
        qp_ref[...] = jnp.where(_iota((lanes, dh), 1) // MOBA_DH == r_half // (2 * t), q_sel, 0.0)

    @pl.when(p < ksteps)
    def _():
        for idx, ref in enumerate(k_refs):
            kp = jnp.concatenate([pair_rows(ref, hp) for hp in range(n_pairs)], axis=1)
            st = _mm_nt(kp, qp_ref[...]) * scale
            st_ref[pl.ds(pl.multiple_of((pps * p + idx) * rows, rows), rows), :] = st
            blk_row = pl.ds(p * (pps // ppb) + idx // ppb, 1)
            grp = kp[0:8, :]
            for r in range(8, rows, 8):
                grp = grp + kp[r:r + 8, :]
            even = _iota(grp.shape, 0) % 2 == 0
            s0 = jnp.sum(jnp.where(even, grp, 0.0), axis=0, keepdims=True)
            s1 = jnp.sum(jnp.where(even, 0.0, grp), axis=0, keepdims=True)
            if idx % ppb == 0:
                ksum0_ref[blk_row, :] = s0
                ksum1_ref[blk_row, :] = s1
            else:
                ksum0_ref[blk_row, :] = ksum0_ref[blk_row, :] + s0
                ksum1_ref[blk_row, :] = ksum1_ref[blk_row, :] + s1

    @pl.when(p == ksteps - 1)
    def _():
        inv_blk = 1.0 / MOBA_BLOCK
        g0 = _mm_f32(ksum0_ref[...] * inv_blk, qp_ref[...], nt=True)
        g1 = _mm_f32(ksum1_ref[...] * inv_blk, qp_ref[...], nt=True)
        gate = jnp.where(lane_half(g0.shape) == 0, g0, g1)
        sel = _select_topk(gate)
        s_own = _mm_nt(kn_ref[...], qf_ref[...]) * scale
        s_own = jnp.where(_iota((t, lanes), 0) <= _iota((t, lanes), 1) % t, s_own, NEG_INF)
        m = jnp.max(s_own, axis=0, keepdims=True)
        brows = 2 * rows
        half_bias = jnp.where(_iota((brows, lanes), 0) % 2 == lane_half((brows, lanes)), 0.0, NEG_INF)

        def masked(j):
            bias = jnp.where(sel[j:j + 1, :] > 0.0, 0.0, NEG_INF)
            return st_ref[j * brows:(j + 1) * brows, :] + (half_bias + bias)

        for j in range(nblk):
            m = jnp.maximum(m, jnp.max(masked(j), axis=0, keepdims=True))
        e_own = jnp.exp(s_own - m)
        l = jnp.sum(e_own, axis=0, keepdims=True)
        for j in range(nblk):
            ej = jnp.exp(masked(j) - m)
            p_ref[j * brows:(j + 1) * brows, :] = ej
            l = l + jnp.sum(ej, axis=0, keepdims=True)
        linv_ref[...] = jnp.broadcast_to(1.0 / l, (lanes, lanes)).T
        pown_ref[...] = jnp.concatenate([e_own, jnp.zeros((PAGE_SIZE - t, lanes), F32)], axis=0)
        acc_ref[...] = jnp.zeros_like(acc_ref)

    @pl.when(p >= ksteps)
    def _():
        for idx, ref in enumerate(v_refs):
            off = pl.multiple_of((pps * (p - ksteps) + idx) * rows, rows)
            prob_t = p_ref[pl.ds(off, rows), :].T
            for hp in range(n_pairs):
                lo, hi = hp * 2 * t, (hp + 1) * 2 * t
                acc_ref[lo:hi, :] = acc_ref[lo:hi, :] + _mm(prob_t[lo:hi, :], pair_rows(ref, hp))

    @pl.when(p == 2 * ksteps - 1)
    def _():
        v_own = jnp.concatenate([vn_ref[...], jnp.zeros((PAGE_SIZE - t, d), F32)], axis=0)
        pown_t = pown_ref[...].T
        for hd in range(MOBA_HEADS):
            out_h = acc_ref[lane_rows(hd), :] + _mm(pown_t[lane_rows(hd), :], v_own[:, hd * MOBA_DH:(hd + 1) * MOBA_DH])
            o_ref[:, hd * MOBA_DH:(hd + 1) * MOBA_DH] = (out_h * linv_ref[lane_rows(hd), :]).astype(o_ref.dtype)


def _moba_sample(h, cache_k, cache_v, page_table, layer, n_seq, t):
    m = h.shape[0]
    d = MOBA_HEADS * MOBA_DH
    n_pages = page_table.shape[1]
    nblk = n_pages * PAGE_SIZE // MOBA_BLOCK
    pps = SAMPLE_PAGES_PER_STEP
    assert pps % (MOBA_BLOCK // PAGE_SIZE) == 0 and n_pages % pps == 0
    ksteps = n_pages // pps
    hq = MOBA_HEADS * t
    assert hq == LANES
    ck = cache_k.reshape(cache_k.shape[0], cache_k.shape[1], PAGE_SIZE * MOBA_HEADS, MOBA_DH)
    cv = cache_v.reshape(cache_v.shape[0], cache_v.shape[1], PAGE_SIZE * MOBA_HEADS, MOBA_DH)

    def kpage(idx):
        return lambda s, p, pt: (layer, pt[s, pps * jnp.minimum(p, ksteps - 1) + idx], 0, 0)

    def vpage(idx):
        return lambda s, p, pt: (layer, pt[s, pps * jnp.maximum(p - ksteps, 0) + idx], 0, 0)

    page_spec = lambda f: pl.BlockSpec((None, None, PAGE_SIZE * MOBA_HEADS, MOBA_DH), f)
    grid_spec = pltpu.PrefetchScalarGridSpec(
        num_scalar_prefetch=1,
        grid=(n_seq, 2 * ksteps),
        in_specs=[
            pl.BlockSpec((t, d), lambda s, p, pt: (s, H_QM // d)),
            pl.BlockSpec((t, d), lambda s, p, pt: (s, H_KM // d)),
            pl.BlockSpec((t, d), lambda s, p, pt: (s, H_VM // d)),
            *[page_spec(kpage(i)) for i in range(pps)],
            *[page_spec(vpage(i)) for i in range(pps)],
        ],
        out_specs=pl.BlockSpec((t, d), lambda s, p, pt: (s, 0)),
        scratch_shapes=[
            pltpu.VMEM((hq, d // 2), F32),
            pltpu.VMEM((hq, d), F32),
            pltpu.VMEM((n_pages * 2 * PAGE_SIZE, hq), F32),
            pltpu.VMEM((n_pages * 2 * PAGE_SIZE, hq), F32),
            pltpu.VMEM((PAGE_SIZE, hq), F32),
            pltpu.VMEM((nblk, d // 2), F32),
            pltpu.VMEM((nblk, d // 2), F32),
            pltpu.VMEM((hq, MOBA_DH), F32),
            pltpu.VMEM((hq, hq), F32),
        ],
    )
    return pl.pallas_call(
        _moba_sample_kernel,
        grid_spec=grid_spec,
        out_shape=jax.ShapeDtypeStruct((m, d), F32),
        compiler_params=_params("parallel", "arbitrary"),
        name="moba_sample",
    )(page_table, h, h, h, *([ck] * pps), *([cv] * pps))


def _merge_kernel(og_ref, om_ref, ga_ref, gb_ref, wa_ref, wb_ref, m_ref):
    ya = _mm(og_ref[...], wa_ref[...])
    yb = _mm(om_ref[...], wb_ref[...])
    m_ref[...] = (jax.nn.sigmoid(ga_ref[...]) * ya + jax.nn.sigmoid(gb_ref[...]) * yb).astype(m_ref.dtype)


def _merge(og, om, h, w_a, w_b, *, tn=512):
    m, d = og.shape
    tm = ROW_TILE
    return pl.pallas_call(
        _merge_kernel,
        grid=(m // tm, d // tn),
        in_specs=[
            pl.BlockSpec((tm, d), lambda i, j: (i, 0)),
            pl.BlockSpec((tm, d), lambda i, j: (i, 0)),
            pl.BlockSpec((tm, tn), lambda i, j: (i, H_GA // tn + j)),
            pl.BlockSpec((tm, tn), lambda i, j: (i, H_GB // tn + j)),
            pl.BlockSpec((d, tn), lambda i, j: (0, j)),
            pl.BlockSpec((d, tn), lambda i, j: (0, j)),
        ],
        out_specs=pl.BlockSpec((tm, tn), lambda i, j: (i, j)),
        out_shape=jax.ShapeDtypeStruct((m, d), MXU_DTYPE),
        compiler_params=_params("parallel", "arbitrary"),
        name="merge",
    )(og, om, h, h, w_a, w_b)


def _outproj_kernel(x_ref, m_ref, wo_ref, g_ref, *rest):
    x = x_ref[...] + jnp.dot(m_ref[...], wo_ref[...], preferred_element_type=F32)
    ms = jnp.mean(x * x, axis=-1, keepdims=True)
    hn = (x * lax.rsqrt(ms + RMS_EPS)) * g_ref[...]
    if len(rest) == 2:
        xo_ref, hn_ref = rest
    else:
        router_ref, xo_ref, hn_ref, logit_ref = rest
        logit_ref[...] = _mm_f32(hn, router_ref[...])
    xo_ref[...] = x
    hn_ref[...] = hn.astype(hn_ref.dtype)


def _outproj(x, mix, w_o, g2, router_pad=None):
    m, d = x.shape
    tm = ROW_TILE
    row_spec = lambda w: pl.BlockSpec((tm, w), lambda i: (i, 0))
    in_specs = [row_spec(d), row_spec(d), pl.BlockSpec((d, d), lambda i: (0, 0)), pl.BlockSpec((1, d), lambda i: (0, 0))]
    out_specs = [row_spec(d), row_spec(d)]
    out_shape = [jax.ShapeDtypeStruct((m, d), F32), jax.ShapeDtypeStruct((m, d), MXU_DTYPE)]
    args = [x, mix, w_o, g2]
    if router_pad is not None:
        in_specs.append(pl.BlockSpec((d, LANES), lambda i: (0, 0)))
        out_specs.append(row_spec(LANES))
        out_shape.append(jax.ShapeDtypeStruct((m, LANES), F32))
        args.append(router_pad)
    return pl.pallas_call(
        _outproj_kernel,
        grid=(m // tm,),
        in_specs=in_specs,
        out_specs=out_specs,
        out_shape=out_shape,
        compiler_params=_params("parallel"),
        name="outproj",
    )(*args)


def _ffn_kernel(x_ref, hn_ref, wg_ref, wu_ref, wd_ref, o_ref):
    @pl.when(pl.program_id(1) == 0)
    def _():
        o_ref[...] = x_ref[...]

    hn = hn_ref[...]
    gate = jnp.dot(hn, wg_ref[...], preferred_element_type=F32)
    up = jnp.dot(hn, wu_ref[...], preferred_element_type=F32)
    act = (gate * jax.nn.sigmoid(gate)) * up
    o_ref[...] += _mm(act, wd_ref[...])


def _ffn(x, hn, wg, wu, wd, *, tf=512):
    m, d = x.shape
    f = wg.shape[1]
    tm = ROW_TILE
    return pl.pallas_call(
        _ffn_kernel,
        grid=(m // tm, f // tf),
        in_specs=[
            pl.BlockSpec((tm, d), lambda i, j: (i, 0)),
            pl.BlockSpec((tm, d), lambda i, j: (i, 0)),
            pl.BlockSpec((d, tf), lambda i, j: (0, j)),
            pl.BlockSpec((d, tf), lambda i, j: (0, j)),
            pl.BlockSpec((tf, d), lambda i, j: (j, 0)),
        ],
        out_specs=pl.BlockSpec((tm, d), lambda i, j: (i, 0)),
        out_shape=jax.ShapeDtypeStruct((m, d), F32),
        compiler_params=_params("parallel", "arbitrary"),
        name="ffn",
    )(x, hn, wg, wu, wd)


def _moe_kernel(x_ref, hn_ref, logit_ref, wg_ref, wu_ref, wd_ref, o_ref, gates_ref, acc_ref):
    e, j = pl.program_id(1), pl.program_id(2)
    nj = pl.num_programs(2)

    @pl.when((e == 0) & (j == 0))
    def _():
        o_ref[...] = x_ref[...]
        lg = logit_ref[...]
        lane = _iota(lg.shape, 1)
        lg = jnp.where(lane < N_EXPERTS, lg, NEG_INF)
        m1 = jnp.max(lg, axis=-1, keepdims=True)
        i1 = jnp.min(jnp.where(lg == m1, lane, LANES), axis=-1, keepdims=True)
        lg2 = jnp.where(lane == i1, NEG_INF, lg)
        m2 = jnp.max(lg2, axis=-1, keepdims=True)
        i2 = jnp.min(jnp.where(lg2 == m2, lane, LANES), axis=-1, keepdims=True)
        e2 = jnp.exp(m2 - m1)
        w1 = 1.0 / (1.0 + e2)
        w2 = e2 / (1.0 + e2)
        gates_ref[...] = jnp.where(lane == i1, w1, 0.0) + jnp.where(lane == i2, w2, 0.0)

    hn = hn_ref[...]
    gate = jnp.dot(hn, wg_ref[...], preferred_element_type=F32)
    up = jnp.dot(hn, wu_ref[...], preferred_element_type=F32)
    act = (gate * jax.nn.sigmoid(gate)) * up
    y = _mm(act, wd_ref[...])

    @pl.when(j == 0)
    def _():
        acc_ref[...] = y

    @pl.when(j > 0)
    def _():
        acc_ref[...] += y

    @pl.when(j == nj - 1)
    def _():
        gates = gates_ref[...]
        ge = jnp.sum(jnp.where(_iota(gates.shape, 1) == e, gates, 0.0), axis=-1, keepdims=True)
        o_ref[...] += ge * acc_ref[...]


def _moe(x, hn, logits, wg, wu, wd, *, tf=768):
    m, d = x.shape
    ne, _, f = wg.shape
    tm = ROW_TILE
    return pl.pallas_call(
        _moe_kernel,
        grid=(m // tm, ne, f // tf),
        in_specs=[
            pl.BlockSpec((tm, d), lambda i, e, j: (i, 0)),
            pl.BlockSpec((tm, d), lambda i, e, j: (i, 0)),
            pl.BlockSpec((tm, LANES), lambda i, e, j: (i, 0)),
            pl.BlockSpec((None, d, tf), lambda i, e, j: (e, 0, j)),
            pl.BlockSpec((None, d, tf), lambda i, e, j: (e, 0, j)),
            pl.BlockSpec((None, tf, d), lambda i, e, j: (e, j, 0)),
        ],
        out_specs=pl.BlockSpec((tm, d), lambda i, e, j: (i, 0)),
        out_shape=jax.ShapeDtypeStruct((m, d), F32),
        scratch_shapes=[pltpu.VMEM((tm, LANES), F32), pltpu.VMEM((tm, d), F32)],
        compiler_params=_params("parallel", "arbitrary", "arbitrary"),
        name="moe",
    )(x, hn, logits, wg, wu, wd)


def _norm_kernel(x_ref, g_ref, o_ref):
    x = x_ref[...]
    ms = jnp.mean(x * x, axis=-1, keepdims=True)
    o_ref[...] = (x * lax.rsqrt(ms + RMS_EPS)) * g_ref[...]


def _final_norm(x, g):
    m, d = x.shape
    tm = ROW_TILE
    return pl.pallas_call(
        _norm_kernel,
        grid=(m // tm,),
        in_specs=[pl.BlockSpec((tm, d), lambda i: (i, 0)), pl.BlockSpec((1, d), lambda i: (0, 0))],
        out_specs=pl.BlockSpec((tm, d), lambda i: (i, 0)),
        out_shape=jax.ShapeDtypeStruct((m, d), F32),
        compiler_params=_params("parallel"),
        name="final_norm",
    )(x, g)


def _pad_to(a, axis, size):
    pad = [(0, 0)] * a.ndim
    pad[axis] = (0, size - a.shape[axis])
    return jnp.pad(a, pad)


def kernel(x_prompt, x_sample, cache_k, cache_v, state_gla, page_table, norm1_g, w_in, gla_w_a2, gla_b_a,
           gla_norm_g, w_gla_br, w_moba_br, w_out, norm2_g, w_gate_d, w_up_d, w_down_d, router,
           w_gate_e, w_up_e, w_down_e, final_g):
    batch, seq, d = x_prompt.shape
    n_seq, t_new, _ = x_sample.shape
    depth = w_in.shape[0]
    lr_lo = 2 * GLA_HEADS * GLA_DK + GLA_HEADS * GLA_DV
    lr_hi = lr_lo + GLA_RANK
    f_pad = -(-w_gate_e.shape[-1] // 768) * 768

    xp = x_prompt.reshape(batch * seq, d)
    xs = x_sample.reshape(n_seq * t_new, d)
    k_p, v_p, s_p, k_s, v_s, s_s = [], [], [], [], [], []
    for l in range(depth):
        bf = lambda a: a.astype(MXU_DTYPE)
        w_main = bf(jnp.concatenate([w_in[l, :, :lr_lo], w_in[l, :, lr_hi:]], axis=1))
        w_lr = bf(_pad_to(w_in[l, :, lr_lo:lr_hi], 1, LANES))
        w_a2 = bf(_pad_to(gla_w_a2[l], 0, LANES))
        b_a = gla_b_a[l][None, :]
        g1 = norm1_g[l][None, :]
        g2 = norm2_g[l][None, :]
        gain = gla_norm_g[l][None, :]
        w_a, w_b, w_o = bf(w_gla_br[l]), bf(w_moba_br[l]), bf(w_out[l])
        is_moe = l % 2 == 1
        li = l // 2
        if is_moe:
            router_pad = _pad_to(router[li], 1, LANES)
            wg = bf(_pad_to(w_gate_e[li], 2, f_pad))
            wu = bf(_pad_to(w_up_e[li], 2, f_pad))
            wd = bf(_pad_to(w_down_e[li], 1, f_pad))
        else:
            router_pad = None
            wg, wu, wd = bf(w_gate_d[li]), bf(w_up_d[li]), bf(w_down_d[li])

        def mixer(x, mix):
            if is_moe:
                x_mid, hn, logits = _outproj(x, mix, w_o, g2, router_pad)
                return _moe(x_mid, hn, logits, wg, wu, wd)
            x_mid, hn = _outproj(x, mix, w_o, g2)
            return _ffn(x_mid, hn, wg, wu, wd)

        h, log_a = _inproj(xp, g1, w_main, w_lr, w_a2, b_a)
        og, s_fin = _gla_prompt(h, log_a, gain, batch, seq)
        om = _moba_prompt(h, batch, seq)
        xp = mixer(xp, _merge(og, om, h, w_a, w_b))
        k_p.append(h[:, H_KM:H_VM].reshape(batch, seq, MOBA_HEADS, MOBA_DH))
        v_p.append(h[:, H_VM:H_GA].reshape(batch, seq, MOBA_HEADS, MOBA_DH))
        s_p.append(s_fin)

        h, log_a = _inproj(xs, g1, w_main, w_lr, w_a2, b_a)
        og, s_new = _gla_sample(h, log_a, gain, state_gla, l, n_seq, t_new)
        om = _moba_sample(h, cache_k, cache_v, page_table, l, n_seq, t_new)
        xs = mixer(xs, _merge(og, om, h, w_a, w_b))
        k_s.append(h[:, H_KM:H_VM].reshape(n_seq, t_new, MOBA_HEADS, MOBA_DH))
        v_s.append(h[:, H_VM:H_GA].reshape(n_seq, t_new, MOBA_HEADS, MOBA_DH))
        s_s.append(s_new)

    g_fin = final_g[None, :]
    y_prompt = _final_norm(xp, g_fin).reshape(batch, seq, d)
    y_sample = _final_norm(xs, g_fin).reshape(n_seq, t_new, d)
    return (y_prompt, y_sample, jnp.stack(k_p), jnp.stack(v_p), jnp.stack(s_p),
            jnp.stack(k_s), jnp.stack(v_s), jnp.stack(s_s))
```
